```python
import jax, jax.numpy as jnp
from jax import lax
import numpy as np

D_MODEL = 2048
BATCH = 8
SEQ = 2048
DEPTH = 1
DEC_BATCH = 128
DEC_SEQ = 1
PAST_LEN = 16384
PAGE_SIZE = 128

H_A = 8
Q_LORA = 512
KV_LORA = 512
NOPE_DIM = 128
ROPE_DIM = 64
V_DIM = 128
MLA_SCALE = (NOPE_DIM + ROPE_DIM) ** -0.5
H_B = 8
HD_B = 128
KV_HEADS_B = 2
ROT_B = HD_B // 4
DSA_SCALE = HD_B ** -0.5
H_IDX = 16
IDX_DIM = 64
ROT_IDX = IDX_DIM // 4
IDX_SCALE = IDX_DIM ** -0.5
TOPK_MAX = 256
PEER_HEADS = 8
N_KEYS = 128
N_EXPERTS = N_KEYS * N_KEYS
PEER_DQ = 256
PEER_TOPK = 16
PEER_BLOCK = 128
ROPE_THETA = 500000.0
Q_BLOCK = 128
RMS_EPS = 1e-6
LN_EPS = 1e-5
ALPHA = (2 * DEPTH) ** 0.25
BETA = (8 * DEPTH) ** -0.25

IN_SIZES = (Q_LORA, KV_LORA + ROPE_DIM, H_B * HD_B, KV_HEADS_B * HD_B, KV_HEADS_B * HD_B,
            H_IDX * IDX_DIM, IDX_DIM, H_IDX, 2 * D_MODEL)
IN_COLS = sum(IN_SIZES)
IN_OFFSETS = tuple(int(o) for o in np.cumsum(IN_SIZES)[:-1])

kernel_name = 'hybrid_mla_dsa_peer_deepnorm_step'


def rms_norm(x, g):
    xf = x.astype(jnp.float32)
    y = xf * lax.rsqrt(jnp.mean(xf * xf, axis=-1, keepdims=True) + RMS_EPS)
    return (y * g.astype(jnp.float32)).astype(x.dtype)


def layer_norm(x, g, b):
    xf = x.astype(jnp.float32)
    mu = jnp.mean(xf, axis=-1, keepdims=True)
    var = jnp.mean(jnp.square(xf - mu), axis=-1, keepdims=True)
    return ((xf - mu) * lax.rsqrt(var + LN_EPS) * g.astype(jnp.float32) + b.astype(jnp.float32)).astype(x.dtype)


def partial_rope(x, pos, rot_dim):
    half = rot_dim // 2
    inv = jnp.power(ROPE_THETA, -jnp.arange(half, dtype=jnp.float32) / half)
    ang = pos.astype(jnp.float32)[:, None] * inv[None, :]
    cos = jnp.cos(ang)[:, None, :]
    sin = jnp.sin(ang)[:, None, :]
    xr = x[..., :rot_dim].astype(jnp.float32)
    x1, x2 = xr[..., :half], xr[..., half:]
    rot = jnp.concatenate([x1 * cos - x2 * sin, x2 * cos + x1 * sin], axis=-1).astype(x.dtype)
    return jnp.concatenate([rot, x[..., rot_dim:]], axis=-1)


def project(x, pos, w_in, q_norm, w_uq, kv_norm, w_uk, idx_g, idx_b):
    B, T, _ = x.shape
    z = x @ w_in
    zq, zkv, zbq, zbk, zbv, ziq, zik, ziw, zg = jnp.split(z, IN_OFFSETS, axis=-1)
    qa = jnp.einsum('btr,rhd->bthd', rms_norm(zq, q_norm), w_uq)
    q_rope = partial_rope(qa[..., NOPE_DIM:], pos, ROPE_DIM)
    q_lat = jnp.einsum('bthn,rhn->bthr', qa[..., :NOPE_DIM], w_uk)
    c_kv = rms_norm(zkv[..., :KV_LORA], kv_norm)
    k_rope = partial_rope(zkv[..., KV_LORA:][:, :, None, :], pos, ROPE_DIM)[:, :, 0]
    q_b = partial_rope(zbq.reshape(B, T, H_B, HD_B), pos, ROT_B)
    k_b = partial_rope(zbk.reshape(B, T, KV_HEADS_B, HD_B), pos, ROT_B)
    v_b = zbv.reshape(B, T, KV_HEADS_B, HD_B)
    q_idx = partial_rope(ziq.reshape(B, T, H_IDX, IDX_DIM), pos, ROT_IDX)
    k_idx = partial_rope(layer_norm(zik, idx_g, idx_b)[:, :, None, :], pos, ROT_IDX)[:, :, 0]
    w_idx = ziw * (H_IDX ** -0.5)
    gates = jax.nn.sigmoid(zg.astype(jnp.float32)).astype(x.dtype).reshape(B, T, 2, D_MODEL)
    return q_lat, q_rope, c_kv, k_rope, q_b, k_b, v_b, q_idx, k_idx, w_idx, gates


def mla_attend(q_lat, q_rope, c_kv, k_rope, q_pos):
    L = c_kv.shape[1]
    s = (jnp.einsum('bthr,blr->bhtl', q_lat, c_kv)
         + jnp.einsum('bthp,blp->bhtl', q_rope, k_rope)).astype(jnp.float32) * MLA_SCALE
    mask = jnp.arange(L)[None, :] <= q_pos[:, None]
    p = jax.nn.softmax(jnp.where(mask, s, -jnp.inf), axis=-1).astype(c_kv.dtype)
    return jnp.einsum('bhtl,blr->bthr', p, c_kv)


def dsa_select(q_idx, w_idx, k_idx, q_pos, k_sel):
    L = k_idx.shape[1]
    dots = jnp.einsum('bthd,bld->bthl', q_idx, k_idx).astype(jnp.float32) * IDX_SCALE
    score = jnp.einsum('bth,bthl->btl', w_idx.astype(jnp.float32), jax.nn.relu(dots))
    mask = jnp.arange(L)[None, :] <= q_pos[:, None]
    _, idx = lax.top_k(jnp.where(mask[None], score, -jnp.inf), k_sel)
    valid = idx <= q_pos[None, :, None]
    return idx, valid


def dsa_attend(q, k_sel, v_sel, valid):
    B, T, H, hd = q.shape
    qg = q.reshape(B, T, KV_HEADS_B, H // KV_HEADS_B, hd)
    s = jnp.einsum('btgjd,btkgd->btgjk', qg, k_sel).astype(jnp.float32) * DSA_SCALE
    s = jnp.where(valid[:, :, None, None, :], s, -jnp.inf)
    p = jax.nn.softmax(s, axis=-1).astype(v_sel.dtype)
    return jnp.einsum('btgjk,btkgd->btgjd', p, v_sel).reshape(B, T, H, hd)


def map_query_blocks(fn, q_arrays, q_pos):
    T = q_pos.shape[0]
    nb = T // Q_BLOCK
    def to_blocks(a):
        return jnp.swapaxes(a.reshape(a.shape[0], nb, Q_BLOCK, *a.shape[2:]), 0, 1)
    out = lax.map(lambda args: fn(*args[0], args[1]),
                  (tuple(to_blocks(a) for a in q_arrays), q_pos.reshape(nb, Q_BLOCK)))
    out = jnp.swapaxes(out, 0, 1)
    return out.reshape(out.shape[0], T, *out.shape[3:])


def gather_pages(pool, page_table):
    g = pool[page_table]
    return g.reshape(page_table.shape[0], -1, *pool.shape[2:])


def gather_selected_rows(pool, page_table, new_rows, idx):
    db = idx.shape[0]
    past_len = page_table.shape[1] * PAGE_SIZE
    pi = jnp.minimum(idx, past_len - 1)
    phys = jnp.take_along_axis(page_table, (pi // PAGE_SIZE).reshape(db, -1), axis=1).reshape(idx.shape)
    flat = pool.reshape(-1, *pool.shape[2:])
    past = flat[phys * PAGE_SIZE + pi % PAGE_SIZE]
    ni = jnp.clip(idx - past_len, 0, new_rows.shape[1] - 1)
    new = jax.vmap(lambda rows, i: rows[i])(new_rows, ni)
    is_past = (idx < past_len).reshape(idx.shape + (1,) * (past.ndim - idx.ndim))
    return jnp.where(is_past, past, new)


def merge_branches(o_lat, o_b, gates, w_uv, mla_w_o, dsa_w_o, w_out):
    B, T = o_b.shape[:2]
    ya = jnp.einsum('bthr,rhv->bthv', o_lat, w_uv).reshape(B, T, H_A * V_DIM) @ mla_w_o
    yb = o_b.reshape(B, T, H_B * HD_B) @ dsa_w_o
    return (gates[:, :, 0] * ya + gates[:, :, 1] * yb) @ w_out


def prompt_mixer(x, pos, k_sel, proj_w, merge_w):
    q_lat, q_rope, c_kv, k_rope, q_b, k_b, v_b, q_idx, k_idx, w_idx, gates = project(x, pos, *proj_w)
    take = jax.vmap(lambda rows, i: rows[i])
    def mla_block(ql, qr, p):
        return mla_attend(ql, qr, c_kv, k_rope, p)
    def dsa_block(qb, qi, wi, p):
        idx, valid = dsa_select(qi, wi, k_idx, p, k_sel)
        return dsa_attend(qb, take(k_b, idx), take(v_b, idx), valid)
    o_lat = map_query_blocks(mla_block, (q_lat, q_rope), pos)
    o_b = map_query_blocks(dsa_block, (q_b, q_idx, w_idx), pos)
    return merge_branches(o_lat, o_b, gates, *merge_w), (c_kv, k_rope, k_b, v_b, k_idx)


def sample_mixer(x, pos, k_sel, page_table, c_pool, kr_pool, k_pool, v_pool, ki_pool, proj_w, merge_w):
    q_lat, q_rope, c_kv, k_rope, q_b, k_b, v_b, q_idx, k_idx, w_idx, gates = project(x, pos, *proj_w)
    c_all = jnp.concatenate([gather_pages(c_pool, page_table), c_kv], axis=1)
    kr_all = jnp.concatenate([gather_pages(kr_pool, page_table), k_rope], axis=1)
    o_lat = mla_attend(q_lat, q_rope, c_all, kr_all, pos)
    ki_all = jnp.concatenate([gather_pages(ki_pool, page_table), k_idx], axis=1)
    idx, valid = dsa_select(q_idx, w_idx, ki_all, pos, k_sel)
    k_s = gather_selected_rows(k_pool, page_table, k_b, idx)
    v_s = gather_selected_rows(v_pool, page_table, v_b, idx)
    o_b = dsa_attend(q_b, k_s, v_s, valid)
    return merge_branches(o_lat, o_b, gates, *merge_w), (c_kv, k_rope, k_b, v_b, k_idx)


def peer_ffn(x, w_q, sub_keys, u, v):
    shp = x.shape
    xt = x.reshape(-1, D_MODEL)
    n = xt.shape[0]
    nb = -(-n // PEER_BLOCK)
    xb = jnp.pad(xt, ((0, nb * PEER_BLOCK - n), (0, 0))).reshape(nb, PEER_BLOCK, D_MODEL)
    def block(xs):
        q = (xs @ w_q).reshape(-1, PEER_HEADS, 2, PEER_DQ // 2)
        s = jnp.einsum('thcd,hckd->thck', q, sub_keys).astype(jnp.float32)
        sv, si = lax.top_k(s, PEER_TOPK)
        cand = sv[..., 0, :, None] + sv[..., 1, None, :]
        cidx = si[..., 0, :, None] * N_KEYS + si[..., 1, None, :]
        fv, fi = lax.top_k(cand.reshape(*cand.shape[:2], -1), PEER_TOPK)
        eidx = jnp.take_along_axis(cidx.reshape(*cidx.shape[:2], -1), fi, axis=-1)
        g = jax.nn.softmax(fv, axis=-1)
        h = jax.nn.gelu(jnp.einsum('thkd,td->thk', u[eidx], xs).astype(jnp.float32), approximate=False)
        return jnp.einsum('thk,thkd->td', (g * h).astype(xs.dtype), v[eidx])
    out = lax.map(block, xb).reshape(-1, D_MODEL)[:n]
    return out.reshape(shp)


def post_norm_layer(x, mix, ln1_g, ln1_b, peer_w, ln2_g, ln2_b):
    h = layer_norm(ALPHA * x + mix, ln1_g, ln1_b)
    return layer_norm(ALPHA * h + peer_ffn(h, *peer_w), ln2_g, ln2_b)


def setup_inputs(seed: int = 0) -> dict:
    key = jax.random.key(seed)
    ks = jax.random.split(key, 32)
    n_pages = PAST_LEN // PAGE_SIZE
    n_used = DEC_BATCH * n_pages
    n_pool = n_used + max(1, n_used // 4)
    L = DEPTH
    def nrm(k, shape, scale):
        return jax.random.normal(k, shape, jnp.float32) * scale
    def gain(k, shape):
        return 1.0 + nrm(k, shape, 0.02)
    page_table = jax.random.permutation(ks[7], n_pool)[:n_used].reshape(DEC_BATCH, n_pages).astype(jnp.int32)
    return {
        'x_prompt': nrm(ks[0], (BATCH, SEQ, D_MODEL), 1.0),
        'x_sample': nrm(ks[1], (DEC_BATCH, DEC_SEQ, D_MODEL), 1.0),
        'cache_kv_latent': nrm(ks[2], (L, n_pool, PAGE_SIZE, KV_LORA), 1.0),
        'cache_k_rope': nrm(ks[3], (L, n_pool, PAGE_SIZE, ROPE_DIM), 1.0),
        'cache_k': nrm(ks[4], (L, n_pool, PAGE_SIZE, KV_HEADS_B, HD_B), 1.0),
        'cache_v': nrm(ks[5], (L, n_pool, PAGE_SIZE, KV_HEADS_B, HD_B), 1.0),
        'cache_idx_k': nrm(ks[6], (L, n_pool, PAGE_SIZE, IDX_DIM), 1.0),
        'page_table': page_table,
        'w_in': nrm(ks[8], (L, D_MODEL, IN_COLS), D_MODEL ** -0.5),
        'mla_q_norm': gain(ks[9], (L, Q_LORA)),
        'mla_w_uq': nrm(ks[10], (L, Q_LORA, H_A, NOPE_DIM + ROPE_DIM), Q_LORA ** -0.5),
        'mla_kv_norm': gain(ks[11], (L, KV_LORA)),
        'mla_w_uk': nrm(ks[12], (L, KV_LORA, H_A, NOPE_DIM), KV_LORA ** -0.5),
        'mla_w_uv': nrm(ks[13], (L, KV_LORA, H_A, V_DIM), BETA * KV_LORA ** -0.5),
        'mla_w_o': nrm(ks[14], (L, H_A * V_DIM, D_MODEL), BETA * (H_A * V_DIM) ** -0.5),
        'idx_k_norm_g': gain(ks[15], (L, IDX_DIM)),
        'idx_k_norm_b': nrm(ks[16], (L, IDX_DIM), 0.02),
        'dsa_w_o': nrm(ks[17], (L, H_B * HD_B, D_MODEL), BETA * (H_B * HD_B) ** -0.5),
        'w_out': nrm(ks[18], (L, D_MODEL, D_MODEL), BETA * D_MODEL ** -0.5),
        'ln1_g': gain(ks[19], (L, D_MODEL)),
        'ln1_b': nrm(ks[20], (L, D_MODEL), 0.02),
        'peer_w_q': nrm(ks[21], (L, D_MODEL, PEER_HEADS * PEER_DQ), D_MODEL ** -0.5),
        'peer_sub_keys': nrm(ks[22], (L, PEER_HEADS, 2, N_KEYS, PEER_DQ // 2), (PEER_DQ // 2) ** -0.5),
        'peer_u': nrm(ks[23], (L, N_EXPERTS, D_MODEL), D_MODEL ** -0.5),
        'peer_v': nrm(ks[24], (L, N_EXPERTS, D_MODEL), BETA * PEER_HEADS ** -0.5),
        'ln2_g': gain(ks[25], (L, D_MODEL)),
        'ln2_b': nrm(ks[26], (L, D_MODEL), 0.02),
    }


def reference(x_prompt, x_sample, cache_kv_latent, cache_k_rope, cache_k, cache_v, cache_idx_k, page_table,
              w_in, mla_q_norm, mla_w_uq, mla_kv_norm, mla_w_uk, mla_w_uv, mla_w_o, idx_k_norm_g, idx_k_norm_b,
              dsa_w_o, w_out, ln1_g, ln1_b, peer_w_q, peer_sub_keys, peer_u, peer_v, ln2_g, ln2_b):
    xp, xs = x_prompt, x_sample
    p_pos = jnp.arange(xp.shape[1], dtype=jnp.int32)
    s_pos = PAST_LEN + jnp.arange(xs.shape[1], dtype=jnp.int32)
    k_sel_p = min(TOPK_MAX, xp.shape[1] // 4)
    k_sel_s = min(TOPK_MAX, (PAST_LEN + xs.shape[1]) // 4)
    p_rows, s_rows = [], []
    for l in range(DEPTH):
        proj_w = (w_in[l], mla_q_norm[l], mla_w_uq[l], mla_kv_norm[l], mla_w_uk[l], idx_k_norm_g[l], idx_k_norm_b[l])
        merge_w = (mla_w_uv[l], mla_w_o[l], dsa_w_o[l], w_out[l])
        peer_w = (peer_w_q[l], peer_sub_keys[l], peer_u[l], peer_v[l])
        mix_p, rows_p = prompt_mixer(xp, p_pos, k_sel_p, proj_w, merge_w)
        mix_s, rows_s = sample_mixer(xs, s_pos, k_sel_s, page_table, cache_kv_latent[l], cache_k_rope[l],
                                     cache_k[l], cache_v[l], cache_idx_k[l], proj_w, merge_w)
        xp = post_norm_layer(xp, mix_p, ln1_g[l], ln1_b[l], peer_w, ln2_g[l], ln2_b[l])
        xs = post_norm_layer(xs, mix_s, ln1_g[l], ln1_b[l], peer_w, ln2_g[l], ln2_b[l])
        p_rows.append(rows_p)
        s_rows.append(rows_s)
    p_lat, p_kr, p_k, p_v, p_ki = [jnp.stack(r) for r in zip(*p_rows)]
    s_lat, s_kr, s_k, s_v, s_ki = [jnp.stack(r) for r in zip(*s_rows)]
    return (xp, xs, p_lat, p_kr, p_k, p_v, p_ki, s_lat, s_kr, s_k, s_v, s_ki)
```

```python
import functools
import math

import numpy as np
import jax
import jax.numpy as jnp
from jax import lax
from jax.experimental import pallas as pl
from jax.experimental.pallas import tpu as pltpu

F32 = jnp.float32
BF16 = jnp.bfloat16
I32 = jnp.int32

D_MODEL = 2048
PAGE_SIZE = 128
H_A = 8
Q_LORA = 512
KV_LORA = 512
NOPE_DIM = 128
ROPE_DIM = 64
V_DIM = 128
MLA_SCALE = (NOPE_DIM + ROPE_DIM) ** -0.5
H_B = 8
HD_B = 128
KV_HEADS_B = 2
ROT_B = HD_B // 4
DSA_SCALE = HD_B ** -0.5
H_IDX = 16
IDX_DIM = 64
ROT_IDX = IDX_DIM // 4
IDX_SCALE = IDX_DIM ** -0.5
W_IDX_SCALE = H_IDX ** -0.5
TOPK_MAX = 256
PEER_HEADS = 8
N_KEYS = 128
PEER_DQ = 256
PEER_TOPK = 16
ROPE_THETA = 500000.0
RMS_EPS = 1e-6
LN_EPS = 1e-5
DEPTH = 1
ALPHA = (2 * DEPTH) ** 0.25

IN_SIZES = (Q_LORA, KV_LORA + ROPE_DIM, H_B * HD_B, KV_HEADS_B * HD_B, KV_HEADS_B * HD_B,
            H_IDX * IDX_DIM, IDX_DIM, H_IDX, 2 * D_MODEL)
IN_OFFSETS = tuple(int(o) for o in np.cumsum(IN_SIZES)[:-1])

LANE = 128
INT_MIN = -2 ** 31
VMEM_LIMIT = 56 * 1024 * 1024

O_Q = 0
O_C = O_Q + Q_LORA
O_KR = O_C + KV_LORA
O_KB = O_KR + LANE
O_VB = O_KB + KV_HEADS_B * HD_B
O_KI = O_VB + KV_HEADS_B * HD_B
O_WI = O_KI + LANE
P1_COLS = O_WI + LANE

NT_DIMS = (((1,), (1,)), ((), ()))


def _cparams(sem):
    return pltpu.CompilerParams(dimension_semantics=sem, vmem_limit_bytes=VMEM_LIMIT)


def _const_spec(shape):
    n = len(shape)
    return pl.BlockSpec(shape, lambda *_: (0,) * n)


def _rms(x, g):
    return x * lax.rsqrt(jnp.mean(x * x, axis=-1, keepdims=True) + RMS_EPS) * g


def _layer_norm(x, g, b):
    mu = jnp.mean(x, axis=-1, keepdims=True)
    d = x - mu
    var = jnp.mean(d * d, axis=-1, keepdims=True)
    return d * lax.rsqrt(var + LN_EPS) * g + b


def _rope(x, tab_ref, half):
    w = x.shape[-1]
    return (x * tab_ref[0] + pltpu.roll(x, w - half, 1) * tab_ref[1] + pltpu.roll(x, half, 1) * tab_ref[2])


def _rope_table(pos, width, head_dim, rot):
    half = rot // 2
    inv = jnp.power(ROPE_THETA, -jnp.arange(half, dtype=F32) / half)
    ang = pos.astype(F32)[:, None] * inv[None, :]
    cos, sin = jnp.cos(ang), jnp.sin(ang)
    t = pos.shape[0]
    rest = jnp.zeros((t, head_dim - rot), F32)
    zh = jnp.zeros((t, half), F32)
    c = jnp.concatenate([cos, cos, rest + 1.0], axis=1)
    s1 = jnp.concatenate([-sin, zh, rest], axis=1)
    s2 = jnp.concatenate([zh, sin, rest], axis=1)
    return jnp.stack([jnp.tile(a, (1, width // head_dim)) for a in (c, s1, s2)])


def _sort_key(s):
    bits = pltpu.bitcast(s + 0.0, I32)
    return jnp.where(bits < 0, bits ^ 0x7FFFFFFF, bits)


def _kth_largest_key(key, k):
    kf = float(k)

    def count_ge(c):
        return jnp.sum((key >= c).astype(F32), axis=1, keepdims=True)

    ans = jnp.where(count_ge(jnp.zeros_like(key[:, :1])) >= kf, 0, INT_MIN).astype(I32)

    def body(i, ans):
        cand = ans | jnp.left_shift(jnp.int32(1), 30 - i)
        return jnp.where(count_ge(cand) >= kf, cand, ans)

    return lax.fori_loop(0, 31, body, ans)


def _top_rows(vals, k):
    r = vals.shape[0]
    rid = lax.broadcasted_iota(I32, vals.shape, 0).astype(F32)
    tops, ids = [], []
    for _ in range(k):
        m = jnp.max(vals, axis=0, keepdims=True)
        idx = jnp.min(jnp.where(vals == m, rid, float(r)), axis=0, keepdims=True)
        vals = jnp.where(rid == idx, -jnp.inf, vals)
        tops.append(m)
        ids.append(idx)
    return jnp.concatenate(tops, axis=0), jnp.concatenate(ids, axis=0)


def _proj1_kernel(x_ref, w_ref, qg_ref, kvg_ref, ig_ref, ib_ref, tkr_ref, tkb_ref, tki_ref,
                  qn_ref, ckv_ref, kr_ref, kb_ref, vb_ref, ki_ref, wi_ref,
                  ckvb_ref, krb_ref, kbb_ref, vbb_ref, kib_ref):
    z = jnp.dot(x_ref[...].astype(BF16), w_ref[...], preferred_element_type=F32)
    qn_ref[...] = _rms(z[:, O_Q:O_Q + Q_LORA], qg_ref[...]).astype(BF16)
    ckv = _rms(z[:, O_C:O_C + KV_LORA], kvg_ref[...])
    ckv_ref[...] = ckv
    ckvb_ref[...] = ckv.astype(BF16)
    kr = _rope(z[:, O_KR:O_KR + LANE], tkr_ref, ROPE_DIM // 2)[:, :ROPE_DIM]
    kr_ref[...] = kr
    krb_ref[...] = kr.astype(BF16)
    kb = _rope(z[:, O_KB:O_KB + KV_HEADS_B * HD_B], tkb_ref, ROT_B // 2)
    kb_ref[...] = kb
    kbb_ref[...] = kb.astype(BF16)
    vb = z[:, O_VB:O_VB + KV_HEADS_B * HD_B]
    vb_ref[...] = vb
    vbb_ref[...] = vb.astype(BF16)
    zi = z[:, O_KI:O_KI + LANE]
    real = lax.broadcasted_iota(I32, zi.shape, 1) < IDX_DIM
    mu = jnp.sum(zi, axis=-1, keepdims=True) * (1.0 / IDX_DIM)
    d = jnp.where(real, zi - mu, 0.0)
    var = jnp.sum(d * d, axis=-1, keepdims=True) * (1.0 / IDX_DIM)
    ki = _rope(d * lax.rsqrt(var + LN_EPS) * ig_ref[...] + ib_ref[...], tki_ref, ROT_IDX // 2)[:, :IDX_DIM]
    ki_ref[...] = ki
    kib_ref[...] = ki.astype(BF16)
    wi_ref[...] = z[:, O_WI:O_WI + H_IDX] * W_IDX_SCALE


def _proj2_kernel(x_ref, w_ref, tqb_ref, tqi_ref, qb_ref, qi_ref):
    z = jnp.dot(x_ref[...].astype(BF16), w_ref[...], preferred_element_type=F32)
    nb = H_B * HD_B
    qb_ref[...] = _rope(z[:, :nb], tqb_ref, ROT_B // 2).astype(BF16)
    qi_ref[...] = (_rope(z[:, nb:], tqi_ref, ROT_IDX // 2) * IDX_SCALE).astype(BF16)


def _gate_kernel(x_ref, w_ref, g_ref):
    z = jnp.dot(x_ref[...].astype(BF16), w_ref[...], preferred_element_type=F32)
    g_ref[...] = jax.nn.sigmoid(z)


def _qpath_kernel(qn_ref, wuq_ref, wukt_ref, tqr_ref, ql_ref, qr_ref):
    qa = jnp.dot(qn_ref[...], wuq_ref[...], preferred_element_type=F32)
    n_nope = H_A * NOPE_DIM
    qr = _rope(qa[:, n_nope:], tqr_ref, ROPE_DIM // 2).astype(BF16)
    for h in range(H_A):
        qr_ref[0, h] = qr[:, h * ROPE_DIM:(h + 1) * ROPE_DIM]
        nope = qa[:, h * NOPE_DIM:(h + 1) * NOPE_DIM].astype(BF16)
        ql_ref[0, h] = jnp.dot(nope, wukt_ref[h], preferred_element_type=F32).astype(BF16)


def _tab_spec(tab, tm, tiles_per_batch):
    w = tab.shape[-1]
    if tab.shape[1] == 1:
        return pl.BlockSpec((3, 1, w), lambda i: (0, 0, 0))
    return pl.BlockSpec((3, tm, w), lambda i: (0, i % tiles_per_batch, 0))


def _project(x2, pos, n_batch, weights):
    n = x2.shape[0]
    t = pos.shape[0]
    tm = min(256, n if t == 1 else t)
    tpb = max(1, t // tm)
    w1, w2, wg, qg, kvg, ig, ib, wuq, wukt = weights
    tkr = _rope_table(pos, LANE, LANE, ROPE_DIM)
    tkb = _rope_table(pos, KV_HEADS_B * HD_B, HD_B, ROT_B)
    tki = _rope_table(pos, LANE, LANE, ROT_IDX)
    tqb = _rope_table(pos, H_B * HD_B, HD_B, ROT_B)
    tqi = _rope_table(pos, H_IDX * IDX_DIM, IDX_DIM, ROT_IDX)
    tqr = _rope_table(pos, H_A * ROPE_DIM, ROPE_DIM, ROPE_DIM)
    row = lambda w: pl.BlockSpec((tm, w), lambda i: (i, 0))
    xspec = row(D_MODEL)
    kvw = KV_HEADS_B * HD_B
    outs1 = pl.pallas_call(
        _proj1_kernel,
        grid=(n // tm,),
        in_specs=[xspec, _const_spec((D_MODEL, P1_COLS)), _const_spec((1, Q_LORA)), _const_spec((1, KV_LORA)),
                  _const_spec((1, LANE)), _const_spec((1, LANE)),
                  _tab_spec(tkr, tm, tpb), _tab_spec(tkb, tm, tpb), _tab_spec(tki, tm, tpb)],
        out_specs=[row(Q_LORA), row(KV_LORA), row(ROPE_DIM), row(kvw), row(kvw), row(IDX_DIM), row(H_IDX),
                   row(KV_LORA), row(ROPE_DIM), row(kvw), row(kvw), row(IDX_DIM)],
        out_shape=[jax.ShapeDtypeStruct((n, Q_LORA), BF16), jax.ShapeDtypeStruct((n, KV_LORA), F32),
                   jax.ShapeDtypeStruct((n, ROPE_DIM), F32), jax.ShapeDtypeStruct((n, kvw), F32),
                   jax.ShapeDtypeStruct((n, kvw), F32), jax.ShapeDtypeStruct((n, IDX_DIM), F32),
                   jax.ShapeDtypeStruct((n, H_IDX), F32),
                   jax.ShapeDtypeStruct((n, KV_LORA), BF16), jax.ShapeDtypeStruct((n, ROPE_DIM), BF16),
                   jax.ShapeDtypeStruct((n, kvw), BF16), jax.ShapeDtypeStruct((n, kvw), BF16),
                   jax.ShapeDtypeStruct((n, IDX_DIM), BF16)],
        compiler_params=_cparams(("parallel",)),
        name="proj1",
    )(x2, w1, qg, kvg, ig, ib, tkr, tkb, tki)
    qn, ckv, kr, kb, vb, ki, wi, ckv_b, kr_b, kb_b, vb_b, ki_b = outs1

    qb, qi = pl.pallas_call(
        _proj2_kernel,
        grid=(n // tm,),
        in_specs=[xspec, _const_spec(w2.shape), _tab_spec(tqb, tm, tpb), _tab_spec(tqi, tm, tpb)],
        out_specs=[row(H_B * HD_B), row(H_IDX * IDX_DIM)],
        out_shape=[jax.ShapeDtypeStruct((n, H_B * HD_B), BF16), jax.ShapeDtypeStruct((n, H_IDX * IDX_DIM), BF16)],
        compiler_params=_cparams(("parallel",)),
        name="proj2",
    )(x2, w2, tqb, tqi)

    tn = 1024
    gates = pl.pallas_call(
        _gate_kernel,
        grid=(n // tm, 2 * D_MODEL // tn),
        in_specs=[pl.BlockSpec((tm, D_MODEL), lambda i, j: (i, 0)), pl.BlockSpec((D_MODEL, tn), lambda i, j: (0, j))],
        out_specs=pl.BlockSpec((tm, tn), lambda i, j: (i, j)),
        out_shape=jax.ShapeDtypeStruct((n, 2 * D_MODEL), F32),
        compiler_params=_cparams(("parallel", "arbitrary")),
        name="gates",
    )(x2, wg)

    t_eff = n // n_batch
    ql, qr = pl.pallas_call(
        _qpath_kernel,
        grid=(n // tm,),
        in_specs=[row(Q_LORA), _const_spec(wuq.shape), _const_spec(wukt.shape), _tab_spec(tqr, tm, tpb)],
        out_specs=[pl.BlockSpec((1, H_A, tm, KV_LORA), lambda i: (i // (t_eff // tm), 0, i % (t_eff // tm), 0)),
                   pl.BlockSpec((1, H_A, tm, ROPE_DIM), lambda i: (i // (t_eff // tm), 0, i % (t_eff // tm), 0))],
        out_shape=[jax.ShapeDtypeStruct((n_batch, H_A, t_eff, KV_LORA), BF16),
                   jax.ShapeDtypeStruct((n_batch, H_A, t_eff, ROPE_DIM), BF16)],
        compiler_params=_cparams(("parallel",)),
        name="qpath",
    )(qn, wuq, wukt, tqr)
    return dict(ckv=ckv, kr=kr, kb=kb, vb=vb, ki=ki, wi=wi, ckv_b=ckv_b, kr_b=kr_b, kb_b=kb_b, vb_b=vb_b,
                ki_b=ki_b, qb=qb, qi=qi, gates=gates, ql=ql, qr=qr)


def _mla_prompt_kernel(ql_ref, qr_ref, c_ref, kr_ref, o_ref, m_sc, l_sc, acc_sc, *, tq, tk):
    i, j = pl.program_id(1), pl.program_id(2)
    rows = H_A * tq

    @pl.when(j == 0)
    def _():
        m_sc[...] = jnp.full_like(m_sc, -jnp.inf)
        l_sc[...] = jnp.zeros_like(l_sc)
        acc_sc[...] = jnp.zeros_like(acc_sc)

    @pl.when(j * tk < (i + 1) * tq)
    def _():
        ql = ql_ref[0].reshape(rows, KV_LORA)
        qr = qr_ref[0].reshape(rows, ROPE_DIM)
        c = c_ref[0]
        s = (lax.dot_general(ql, c, NT_DIMS, preferred_element_type=F32)
             + lax.dot_general(qr, kr_ref[0], NT_DIMS, preferred_element_type=F32)) * MLA_SCALE
        q_pos = i * tq + lax.broadcasted_iota(I32, (H_A, tq, tk), 1).reshape(rows, tk)
        k_pos = j * tk + lax.broadcasted_iota(I32, (rows, tk), 1)
        s = jnp.where(k_pos <= q_pos, s, -jnp.inf)
        m_new = jnp.maximum(m_sc[...], jnp.max(s, axis=1, keepdims=True))
        a = jnp.exp(m_sc[...] - m_new)
        p = jnp.exp(s - m_new)
        l_sc[...] = a * l_sc[...] + jnp.sum(p, axis=1, keepdims=True)
        acc_sc[...] = a * acc_sc[...] + jnp.dot(p.astype(BF16), c, preferred_element_type=F32)
        m_sc[...] = m_new

    @pl.when(j == pl.num_programs(2) - 1)
    def _():
        o_ref[0] = (acc_sc[...] / l_sc[...]).astype(BF16).reshape(H_A, tq, KV_LORA)


def _mla_prompt(ql, qr, ckv_b, kr_b):
    b, _, t, _ = ql.shape
    tq = min(128, t)
    tk = min(256, t)
    last = lambda i, j: jnp.minimum(j, ((i + 1) * tq - 1) // tk)
    return pl.pallas_call(
        functools.partial(_mla_prompt_kernel, tq=tq, tk=tk),
        grid=(b, t // tq, t // tk),
        in_specs=[pl.BlockSpec((1, H_A, tq, KV_LORA), lambda bb, i, j: (bb, 0, i, 0)),
                  pl.BlockSpec((1, H_A, tq, ROPE_DIM), lambda bb, i, j: (bb, 0, i, 0)),
                  pl.BlockSpec((1, tk, KV_LORA), lambda bb, i, j: (bb, last(i, j), 0)),
                  pl.BlockSpec((1, tk, ROPE_DIM), lambda bb, i, j: (bb, last(i, j), 0))],
        out_specs=pl.BlockSpec((1, H_A, tq, KV_LORA), lambda bb, i, j: (bb, 0, i, 0)),
        out_shape=jax.ShapeDtypeStruct((b, H_A, t, KV_LORA), BF16),
        scratch_shapes=[pltpu.VMEM((H_A * tq, 1), F32), pltpu.VMEM((H_A * tq, 1), F32),
                        pltpu.VMEM((H_A * tq, KV_LORA), F32)],
        compiler_params=_cparams(("parallel", "parallel", "arbitrary")),
        name="mla_prompt",
    )(ql, qr, ckv_b, kr_b)


def _dsa_prompt_kernel(qi_ref, wi_ref, qb_ref, ki_ref, kb_ref, vb_ref, o_ref, *, tq, k_sel):
    i = pl.program_id(1)
    t_all = ki_ref.shape[1]
    ki = ki_ref[0]
    wi = wi_ref[...]
    score = jnp.zeros((tq, t_all), F32)
    for h in range(H_IDX):
        d = lax.dot_general(qi_ref[:, h * IDX_DIM:(h + 1) * IDX_DIM], ki, NT_DIMS, preferred_element_type=F32)
        score = score + wi[:, h:h + 1] * jnp.maximum(d, 0.0)
    q_pos = i * tq + lax.broadcasted_iota(I32, (tq, t_all), 0)
    k_pos = lax.broadcasted_iota(I32, (tq, t_all), 1)
    causal = k_pos <= q_pos
    key = jnp.where(causal, _sort_key(score), INT_MIN)
    thr = _kth_largest_key(key, k_sel)
    keep = (causal & (key >= thr))[None]
    hpg = H_B // KV_HEADS_B
    for g in range(KV_HEADS_B):
        qg = jnp.concatenate([qb_ref[:, (g * hpg + jj) * HD_B:(g * hpg + jj + 1) * HD_B] for jj in range(hpg)], axis=0)
        kg = kb_ref[0, :, g * HD_B:(g + 1) * HD_B]
        vg = vb_ref[0, :, g * HD_B:(g + 1) * HD_B]
        s = lax.dot_general(qg, kg, NT_DIMS, preferred_element_type=F32) * DSA_SCALE
        s = jnp.where(keep, s.reshape(hpg, tq, t_all), -jnp.inf).reshape(hpg * tq, t_all)
        p = jnp.exp(s - jnp.max(s, axis=1, keepdims=True))
        l = jnp.sum(p, axis=1, keepdims=True)
        o = (jnp.dot(p.astype(BF16), vg, preferred_element_type=F32) / l).astype(BF16)
        for jj in range(hpg):
            o_ref[:, (g * hpg + jj) * HD_B:(g * hpg + jj + 1) * HD_B] = o[jj * tq:(jj + 1) * tq]


def _dsa_prompt(qi, wi, qb, ki_b, kb_b, vb_b, b, t):
    tq = min(128, t)
    k_sel = min(TOPK_MAX, t // 4)
    tpb = t // tq
    row = lambda w: pl.BlockSpec((tq, w), lambda bb, i: (bb * tpb + i, 0))
    per_b = lambda w: pl.BlockSpec((1, t, w), lambda bb, i: (bb, 0, 0))
    kvw = KV_HEADS_B * HD_B
    return pl.pallas_call(
        functools.partial(_dsa_prompt_kernel, tq=tq, k_sel=k_sel),
        grid=(b, tpb),
        in_specs=[row(H_IDX * IDX_DIM), row(H_IDX), row(H_B * HD_B), per_b(IDX_DIM), per_b(kvw), per_b(kvw)],
        out_specs=row(H_B * HD_B),
        out_shape=jax.ShapeDtypeStruct((b * t, H_B * HD_B), BF16),
        compiler_params=_cparams(("parallel", "arbitrary")),
        name="dsa_prompt",
    )(qi, wi, qb, ki_b.reshape(b, t, IDX_DIM), kb_b.reshape(b, t, kvw), vb_b.reshape(b, t, kvw))


def _merge_kernel(ol_ref, ob_ref, g_ref, x_ref, wuv_ref, wao_ref, wbo_ref, wout_ref, lg_ref, lb_ref, h_ref):
    ta = jnp.concatenate([jnp.dot(ol_ref[0, h], wuv_ref[h], preferred_element_type=F32) for h in range(H_A)],
                         axis=1).astype(BF16)
    ya = jnp.dot(ta, wao_ref[...], preferred_element_type=F32)
    yb = jnp.dot(ob_ref[...], wbo_ref[...], preferred_element_type=F32)
    m = (g_ref[:, :D_MODEL] * ya + g_ref[:, D_MODEL:] * yb).astype(BF16)
    mix = jnp.dot(m, wout_ref[...], preferred_element_type=F32)
    h_ref[...] = _layer_norm(ALPHA * x_ref[...] + mix, lg_ref[...], lb_ref[...])


def _merge(o_lat, o_b, gates, x2, weights):
    b, _, t, _ = o_lat.shape
    tm = min(256, t)
    tpb = t // tm
    wuv, wao, wbo, wout, lg, lb = weights
    row = lambda w: pl.BlockSpec((tm, w), lambda i: (i, 0))
    once = lambda a: pl.BlockSpec(a.shape, lambda i: (0,) * a.ndim, pipeline_mode=pl.Buffered(1))
    return pl.pallas_call(
        _merge_kernel,
        grid=(b * tpb,),
        in_specs=[pl.BlockSpec((1, H_A, tm, KV_LORA), lambda i: (i // tpb, 0, i % tpb, 0)),
                  row(H_B * HD_B), row(2 * D_MODEL), row(D_MODEL),
                  once(wuv), once(wao), once(wbo), once(wout), once(lg), once(lb)],
        out_specs=row(D_MODEL),
        out_shape=jax.ShapeDtypeStruct((b * t, D_MODEL), F32),
        compiler_params=_cparams(("parallel",)),
        name="merge",
    )(o_lat, o_b, gates, x2, wuv, wao, wbo, wout, lg, lb)


def _peer_route_kernel(h_ref, wq_ref, sk_ref, e_ref, g_ref):
    q = jnp.dot(h_ref[...].astype(BF16), wq_ref[...], preferred_element_type=F32).astype(BF16)
    half = PEER_DQ // 2
    gates, experts = [], []
    for h in range(PEER_HEADS):
        sv, si = [], []
        for c in range(2):
            qhc = q[:, (2 * h + c) * half:(2 * h + c + 1) * half]
            s_t = lax.dot_general(sk_ref[h, c], qhc, NT_DIMS, preferred_element_type=F32)
            v, ix = _top_rows(s_t, PEER_TOPK)
            sv.append(v)
            si.append(ix)
        cand = jnp.concatenate([sv[0][a:a + 1] + sv[1] for a in range(PEER_TOPK)], axis=0)
        cidx = jnp.concatenate([si[0][a:a + 1] * float(N_KEYS) + si[1] for a in range(PEER_TOPK)], axis=0)
        rid = lax.broadcasted_iota(I32, cand.shape, 0).astype(F32)
        fv, fe = [], []
        for _ in range(PEER_TOPK):
            m = jnp.max(cand, axis=0, keepdims=True)
            pos = jnp.min(jnp.where(cand == m, rid, float(cand.shape[0])), axis=0, keepdims=True)
            hit = rid == pos
            fe.append(jnp.sum(jnp.where(hit, cidx, 0.0), axis=0, keepdims=True))
            cand = jnp.where(hit, -jnp.inf, cand)
            fv.append(m)
        fv = jnp.concatenate(fv, axis=0)
        p = jnp.exp(fv - fv[0:1])
        gates.append(p / jnp.sum(p, axis=0, keepdims=True))
        experts.append(jnp.concatenate(fe, axis=0))
    g_ref[...] = jnp.concatenate(gates, axis=0).T
    e_ref[...] = jnp.concatenate(experts, axis=0).T.astype(I32)


def _peer_route(h2, wq, sk):
    n = h2.shape[0]
    tm = min(128, n)
    once = lambda a: pl.BlockSpec(a.shape, lambda i: (0,) * a.ndim)
    return pl.pallas_call(
        _peer_route_kernel,
        grid=(n // tm,),
        in_specs=[pl.BlockSpec((tm, D_MODEL), lambda i: (i, 0)), once(wq), once(sk)],
        out_specs=[pl.BlockSpec((tm, N_SEL), lambda i: (i, 0)), pl.BlockSpec((tm, N_SEL), lambda i: (i, 0))],
        out_shape=[jax.ShapeDtypeStruct((n, N_SEL), I32), jax.ShapeDtypeStruct((n, N_SEL), F32)],
        compiler_params=_cparams(("parallel",)),
        name="peer_route",
    )(h2, wq, sk)


N_SEL = PEER_HEADS * PEER_TOPK
PEER_TT = 16
PEER_SLOTS = 3


def _peer_expert_kernel(e_hbm, h_ref, g_ref, u_hbm, v_hbm, lg_ref, lb_ref, y_ref,
                        idx_sm, ubuf, vbuf, po_sc, sem_i, sem_u, sem_v):
    i = pl.program_id(0)
    n_steps = pl.num_programs(0)
    tt = PEER_TT

    def idx_copy(step, slot):
        return pltpu.make_async_copy(e_hbm.at[step], idx_sm.at[pl.ds(slot * tt, tt)], sem_i.at[slot])

    @pl.when(i == 0)
    def _():
        idx_copy(0, 0).start()

    cur = i % 2
    idx_copy(i, cur).wait()

    @pl.when(i + 1 < n_steps)
    def _():
        idx_copy(i + 1, 1 - cur).start()

    def row_copy(tab, buf, sem, e, slot, k):
        return pltpu.make_async_copy(tab.at[pl.ds(e, 1)], buf.at[slot, pl.ds(k, 1)], sem.at[slot])

    def issue(t, slot):
        def body(k, carry):
            e = idx_sm[cur * tt + t, k]
            row_copy(u_hbm, ubuf, sem_u, e, slot, k).start()
            row_copy(v_hbm, vbuf, sem_v, e, slot, k).start()
            return carry
        lax.fori_loop(0, N_SEL, body, 0, unroll=8)

    def wait(slot):
        pltpu.make_async_copy(u_hbm.at[pl.ds(0, N_SEL)], ubuf.at[slot], sem_u.at[slot]).wait()
        pltpu.make_async_copy(v_hbm.at[pl.ds(0, N_SEL)], vbuf.at[slot], sem_v.at[slot]).wait()

    ahead = PEER_SLOTS - 1
    for t in range(ahead):
        issue(t, t)
    g_t = g_ref[...].T
    for t in range(tt):
        slot = t % PEER_SLOTS
        if t + ahead < tt:
            issue(t + ahead, (t + ahead) % PEER_SLOTS)
        wait(slot)
        x_row = h_ref[t:t + 1, :]
        hk = jnp.sum(ubuf[slot] * x_row, axis=1, keepdims=True)
        a = g_t[:, t:t + 1] * (0.5 * hk * (1.0 + lax.erf(hk * math.sqrt(0.5))))
        po_sc[t:t + 1, :] = jnp.sum(vbuf[slot] * a, axis=0, keepdims=True)
    y_ref[...] = _layer_norm(ALPHA * h_ref[...] + po_sc[...], lg_ref[...], lb_ref[...])


def _peer_experts(eidx, gate, h2, u, v, lg, lb):
    n = h2.shape[0]
    tt = PEER_TT
    steps = n // tt
    row = lambda w: pl.BlockSpec((tt, w), lambda i: (i, 0))
    anyspec = pl.BlockSpec(memory_space=pl.ANY)
    return pl.pallas_call(
        _peer_expert_kernel,
        grid=(steps,),
        in_specs=[anyspec, row(D_MODEL), row(N_SEL), anyspec, anyspec,
                  _const_spec((1, D_MODEL)), _const_spec((1, D_MODEL))],
        out_specs=row(D_MODEL),
        out_shape=jax.ShapeDtypeStruct((n, D_MODEL), F32),
        scratch_shapes=[pltpu.SMEM((2 * tt, N_SEL), I32),
                        pltpu.VMEM((PEER_SLOTS, N_SEL, D_MODEL), F32),
                        pltpu.VMEM((PEER_SLOTS, N_SEL, D_MODEL), F32),
                        pltpu.VMEM((tt, D_MODEL), F32),
                        pltpu.SemaphoreType.DMA((2,)),
                        pltpu.SemaphoreType.DMA((PEER_SLOTS,)),
                        pltpu.SemaphoreType.DMA((PEER_SLOTS,))],
        compiler_params=_cparams(("arbitrary",)),
        name="peer_experts",
    )(eidx.reshape(steps, tt, N_SEL), h2, gate, u, v, lg, lb)


def _peer_layer(h2, peer_w):
    wq, sk, u, v, lg, lb = peer_w
    eidx, gate = _peer_route(h2, wq, sk)
    return _peer_experts(eidx, gate, h2, u, v, lg, lb)


PAGES_PER_STEP = 8


def _sample_pass1_kernel(pt_ref, ql_ref, qr_ref, qi_ref, wi_ref, cn_ref, krn_ref, *rest):
    pp = PAGES_PER_STEP
    c_refs, kr_refs, ki_refs = rest[:pp], rest[pp:2 * pp], rest[2 * pp:3 * pp]
    o_ref, sc_ref, m_sc, l_sc, acc_sc = rest[3 * pp:]
    p_id = pl.program_id(1)

    @pl.when(p_id == 0)
    def _():
        m_sc[...] = jnp.full_like(m_sc, -jnp.inf)
        l_sc[...] = jnp.zeros_like(l_sc)
        acc_sc[...] = jnp.zeros_like(acc_sc)

    ql, qr = ql_ref[0], qr_ref[0]

    def absorb(s, weighted_values):
        s = s * MLA_SCALE
        m_new = jnp.maximum(m_sc[...], jnp.max(s, axis=1, keepdims=True))
        a = jnp.exp(m_sc[...] - m_new)
        p = jnp.exp(s - m_new)
        l_sc[...] = a * l_sc[...] + jnp.sum(p, axis=1, keepdims=True)
        acc_sc[...] = a * acc_sc[...] + weighted_values(p.astype(BF16))
        m_sc[...] = m_new

    c = jnp.concatenate([r[0].astype(BF16) for r in c_refs], axis=0)
    kr_t = jnp.concatenate([r[0].astype(BF16) for r in kr_refs], axis=1)
    absorb(lax.dot_general(ql, c, NT_DIMS, preferred_element_type=F32)
           + jnp.dot(qr, kr_t, preferred_element_type=F32),
           lambda p: jnp.dot(p, c, preferred_element_type=F32))
    ki_t = jnp.concatenate([r[0].astype(BF16) for r in ki_refs], axis=1)
    d = jnp.dot(qi_ref[0], ki_t, preferred_element_type=F32)
    sc_ref[0] = jnp.sum(wi_ref[0] * jnp.maximum(d, 0.0), axis=0, keepdims=True)

    @pl.when(p_id == pl.num_programs(1) - 1)
    def _():
        cn = cn_ref[0].astype(F32)
        s_new = (jnp.sum(ql.astype(F32) * cn, axis=1, keepdims=True)
                 + jnp.sum(qr.astype(F32) * krn_ref[0].astype(F32), axis=1, keepdims=True))
        absorb(s_new, lambda p: p.astype(F32) * cn)
        o_ref[0] = (acc_sc[...] / l_sc[...]).astype(BF16)


def _sample_pass1(page_table, ql, qr, qi3, wi3, ckv_new, kr_new, c_pool, kr_pool, ki_pool):
    db, n_pages = page_table.shape
    pp = PAGES_PER_STEP
    per_b = lambda a: pl.BlockSpec((1,) + a.shape[1:], lambda b, p, pt: (b,) + (0,) * (a.ndim - 1))

    def page_specs(*page_shape):
        return [pl.BlockSpec((1,) + page_shape, lambda b, p, pt, j=j: (pt[b, p * pp + j], 0, 0))
                for j in range(pp)]

    grid_spec = pltpu.PrefetchScalarGridSpec(
        num_scalar_prefetch=1,
        grid=(db, n_pages // pp),
        in_specs=[per_b(ql), per_b(qr), per_b(qi3), per_b(wi3), per_b(ckv_new), per_b(kr_new)]
                 + page_specs(PAGE_SIZE, KV_LORA) + page_specs(ROPE_DIM, PAGE_SIZE) + page_specs(IDX_DIM, PAGE_SIZE),
        out_specs=[pl.BlockSpec((1, H_A, KV_LORA), lambda b, p, pt: (b, 0, 0)),
                   pl.BlockSpec((1, 1, pp * PAGE_SIZE), lambda b, p, pt: (b, 0, p))],
        scratch_shapes=[pltpu.VMEM((H_A, 1), F32), pltpu.VMEM((H_A, 1), F32), pltpu.VMEM((H_A, KV_LORA), F32)],
    )
    return pl.pallas_call(
        _sample_pass1_kernel,
        grid_spec=grid_spec,
        out_shape=[jax.ShapeDtypeStruct((db, H_A, KV_LORA), BF16),
                   jax.ShapeDtypeStruct((db, 1, n_pages * PAGE_SIZE), F32)],
        compiler_params=_cparams(("parallel", "arbitrary")),
        name="sample_pass1",
    )(page_table, ql, qr, qi3, wi3, ckv_new, kr_new, *([c_pool] * pp), *([kr_pool] * pp), *([ki_pool] * pp))


def _sample_select_kernel(sc_ref, qi_ref, wi_ref, kin_ref, thr_ref, new_ref, *, k_sel):
    past = sc_ref[...]
    qi = qi_ref[...].astype(F32)
    kn = kin_ref[...].astype(F32)
    d = jnp.sum(qi * kn, axis=2)
    s_new = jnp.sum(wi_ref[...] * jnp.maximum(d, 0.0), axis=1, keepdims=True)
    key_new = _sort_key(s_new)
    lane0 = lax.broadcasted_iota(I32, (past.shape[0], LANE), 1) == 0
    key = jnp.concatenate([_sort_key(past), jnp.where(lane0, key_new, INT_MIN)], axis=1)
    thr = _kth_largest_key(key, k_sel)
    thr_ref[...] = thr
    new_ref[...] = (key_new >= thr).astype(I32)


def _sample_select(scores, qi3, wi2, ki_new3, k_sel):
    db, past = scores.shape
    rows = min(8, db)
    return pl.pallas_call(
        functools.partial(_sample_select_kernel, k_sel=k_sel),
        grid=(db // rows,),
        in_specs=[pl.BlockSpec((rows, past), lambda i: (i, 0)),
                  pl.BlockSpec((rows, H_IDX, IDX_DIM), lambda i: (i, 0, 0)),
                  pl.BlockSpec((rows, H_IDX), lambda i: (i, 0)),
                  pl.BlockSpec((rows, 1, IDX_DIM), lambda i: (i, 0, 0))],
        out_specs=[pl.BlockSpec((rows, 1), lambda i: (i, 0)), pl.BlockSpec((rows, 1), lambda i: (i, 0))],
        out_shape=[jax.ShapeDtypeStruct((db, 1), I32), jax.ShapeDtypeStruct((db, 1), I32)],
        compiler_params=_cparams(("parallel",)),
        name="sample_select",
    )(scores, qi3, wi2, ki_new3)


def _sample_pass2_kernel(pt_ref, thr_ref, new_ref, qb_ref, sc_ref, kn_ref, vn_ref, *rest):
    pp = PAGES_PER_STEP
    k_refs, v_refs = rest[:pp], rest[pp:2 * pp]
    o_ref, m_sc, l_sc, acc_sc = rest[2 * pp:]
    b, p_id = pl.program_id(0), pl.program_id(1)
    hpg = H_B // KV_HEADS_B

    @pl.when(p_id == 0)
    def _():
        m_sc[...] = jnp.full_like(m_sc, -jnp.inf)
        l_sc[...] = jnp.zeros_like(l_sc)
        acc_sc[...] = jnp.zeros_like(acc_sc)

    def absorb(g, s, keep, weighted_values):
        rows = slice(g * hpg, (g + 1) * hpg)
        s = jnp.where(keep, s * DSA_SCALE, -jnp.inf)
        m_old = m_sc[rows]
        m_new = jnp.maximum(m_old, jnp.max(s, axis=1, keepdims=True))
        m_safe = jnp.where(m_new == -jnp.inf, 0.0, m_new)
        a = jnp.exp(m_old - m_safe)
        p = jnp.exp(s - m_safe)
        l_sc[rows] = a * l_sc[rows] + jnp.sum(p, axis=1, keepdims=True)
        acc_sc[rows] = a * acc_sc[rows] + weighted_values(p.astype(BF16))
        m_sc[rows] = m_new

    keep = _sort_key(sc_ref[0]) >= thr_ref[b]
    for g in range(KV_HEADS_B):
        own = pl.ds(g, PAGE_SIZE, stride=KV_HEADS_B)
        k = jnp.concatenate([r[0, own, :].astype(BF16) for r in k_refs], axis=0)
        v = jnp.concatenate([r[0, own, :].astype(BF16) for r in v_refs], axis=0)
        absorb(g, lax.dot_general(qb_ref[0, g * hpg:(g + 1) * hpg], k, NT_DIMS, preferred_element_type=F32),
               keep, lambda p, v=v: jnp.dot(p, v, preferred_element_type=F32))

    @pl.when(p_id == pl.num_programs(1) - 1)
    def _():
        for g in range(KV_HEADS_B):
            cols = slice(g * HD_B, (g + 1) * HD_B)
            kn = kn_ref[0, :, cols].astype(F32)
            vn = vn_ref[0, :, cols].astype(F32)
            s_new = jnp.sum(qb_ref[0, g * hpg:(g + 1) * hpg].astype(F32) * kn, axis=1, keepdims=True)
            absorb(g, s_new, new_ref[b] > 0, lambda p, vn=vn: p.astype(F32) * vn)
        o_ref[0] = (acc_sc[...] / l_sc[...]).astype(BF16)


def _sample_pass2(page_table, thr, sel_new, qb3, scores3, kb_new, vb_new, k_pool, v_pool):
    db, n_pages = page_table.shape
    pp = PAGES_PER_STEP
    kvw = KV_HEADS_B * HD_B
    per_b = lambda a: pl.BlockSpec((1,) + a.shape[1:], lambda b, p, *_: (b,) + (0,) * (a.ndim - 1))
    pages = [pl.BlockSpec((1, PAGE_SIZE * KV_HEADS_B, HD_B), lambda b, p, pt, th, nw, j=j: (pt[b, p * pp + j], 0, 0))
             for j in range(pp)]
    grid_spec = pltpu.PrefetchScalarGridSpec(
        num_scalar_prefetch=3,
        grid=(db, n_pages // pp),
        in_specs=[per_b(qb3), pl.BlockSpec((1, 1, pp * PAGE_SIZE), lambda b, p, *_: (b, 0, p)),
                  per_b(kb_new), per_b(vb_new)] + pages + pages,
        out_specs=pl.BlockSpec((1, H_B, HD_B), lambda b, p, *_: (b, 0, 0)),
        scratch_shapes=[pltpu.VMEM((H_B, 1), F32), pltpu.VMEM((H_B, 1), F32), pltpu.VMEM((H_B, HD_B), F32)],
    )
    return pl.pallas_call(
        _sample_pass2_kernel,
        grid_spec=grid_spec,
        out_shape=jax.ShapeDtypeStruct((db, H_B, HD_B), BF16),
        compiler_params=_cparams(("parallel", "arbitrary")),
        name="sample_pass2",
    )(page_table, thr, sel_new, qb3, scores3, kb_new, vb_new, *([k_pool] * pp), *([v_pool] * pp))


def kernel(x_prompt, x_sample, cache_kv_latent, cache_k_rope, cache_k, cache_v, cache_idx_k, page_table, w_in, mla_q_norm, mla_w_uq, mla_kv_norm, mla_w_uk, mla_w_uv, mla_w_o, idx_k_norm_g, idx_k_norm_b, dsa_w_o, w_out, ln1_g, ln1_b, peer_w_q, peer_sub_keys, peer_u, peer_v, ln2_g, ln2_b):
    b, t, _ = x_prompt.shape
    db, ts, _ = x_sample.shape
    assert ts == 1, "the sample group decodes one token per sequence"
    n_pages = page_table.shape[1]
    past_len = n_pages * PAGE_SIZE
    l = 0

    wq_, wkv_, wbq_, wbk_, wbv_, wiq_, wik_, wiw_, wg_ = jnp.split(w_in[l], IN_OFFSETS, axis=1)
    padc = lambda a: jnp.pad(a, ((0, 0), (0, LANE - a.shape[1])))
    w1 = jnp.concatenate([wq_, wkv_[:, :KV_LORA], padc(wkv_[:, KV_LORA:]), wbk_, wbv_, padc(wik_), padc(wiw_)],
                         axis=1).astype(BF16)
    w2 = jnp.concatenate([wbq_, wiq_], axis=1).astype(BF16)
    wg = wg_.astype(BF16)
    wuq = jnp.concatenate([mla_w_uq[l][:, :, :NOPE_DIM].reshape(Q_LORA, H_A * NOPE_DIM),
                           mla_w_uq[l][:, :, NOPE_DIM:].reshape(Q_LORA, H_A * ROPE_DIM)], axis=1).astype(BF16)
    wukt = jnp.transpose(mla_w_uk[l], (1, 2, 0)).astype(BF16)
    row1 = lambda a: a.reshape(1, -1).astype(F32)
    pad1 = lambda a: jnp.pad(a.astype(F32), (0, LANE - a.shape[0])).reshape(1, LANE)
    proj_w = (w1, w2, wg, row1(mla_q_norm[l]), row1(mla_kv_norm[l]), pad1(idx_k_norm_g[l]), pad1(idx_k_norm_b[l]),
              wuq, wukt)
    merge_w = (jnp.transpose(mla_w_uv[l], (1, 0, 2)).astype(BF16), mla_w_o[l].astype(BF16),
               dsa_w_o[l].astype(BF16), w_out[l].astype(BF16), row1(ln1_g[l]), row1(ln1_b[l]))
    peer_w = (peer_w_q[l].astype(BF16), peer_sub_keys[l].astype(BF16), peer_u[l], peer_v[l],
              row1(ln2_g[l]), row1(ln2_b[l]))
    kvw = KV_HEADS_B * HD_B

    xp2 = x_prompt.reshape(b * t, D_MODEL)
    pp_ = _project(xp2, jnp.arange(t, dtype=I32), b, proj_w)
    o_lat_p = _mla_prompt(pp_["ql"], pp_["qr"], pp_["ckv_b"].reshape(b, t, KV_LORA), pp_["kr_b"].reshape(b, t, ROPE_DIM))
    o_b_p = _dsa_prompt(pp_["qi"], pp_["wi"], pp_["qb"], pp_["ki_b"], pp_["kb_b"], pp_["vb_b"], b, t)
    h_p = _merge(o_lat_p, o_b_p, pp_["gates"], xp2, merge_w)
    y_p = _peer_layer(h_p, peer_w).reshape(b, t, D_MODEL)

    xs2 = x_sample.reshape(db, D_MODEL)
    sp = _project(xs2, jnp.full((1,), past_len, I32), 1, proj_w)
    ql_s = jnp.transpose(sp["ql"][0], (1, 0, 2))
    qr_s = jnp.transpose(sp["qr"][0], (1, 0, 2))
    qi3 = sp["qi"].reshape(db, H_IDX, IDX_DIM)
    o_lat_s, scores = _sample_pass1(
        page_table, ql_s, qr_s, qi3, sp["wi"].reshape(db, H_IDX, 1),
        sp["ckv_b"].reshape(db, 1, KV_LORA), sp["kr_b"].reshape(db, 1, ROPE_DIM),
        cache_kv_latent[l], jnp.swapaxes(cache_k_rope[l], 1, 2), jnp.swapaxes(cache_idx_k[l], 1, 2))
    k_sel_s = min(TOPK_MAX, (past_len + 1) // 4)
    thr, sel_new = _sample_select(scores.reshape(db, past_len), qi3, sp["wi"], sp["ki_b"].reshape(db, 1, IDX_DIM),
                                  k_sel_s)
    o_b_s = _sample_pass2(page_table, thr.reshape(db), sel_new.reshape(db), sp["qb"].reshape(db, H_B, HD_B), scores,
                          sp["kb_b"].reshape(db, 1, kvw), sp["vb_b"].reshape(db, 1, kvw),
                          cache_k[l].reshape(-1, PAGE_SIZE * KV_HEADS_B, HD_B),
                          cache_v[l].reshape(-1, PAGE_SIZE * KV_HEADS_B, HD_B))
    o_lat_s4 = jnp.transpose(o_lat_s, (1, 0, 2)).reshape(1, H_A, db, KV_LORA)
    h_s = _merge(o_lat_s4, o_b_s.reshape(db, H_B * HD_B), sp["gates"], xs2, merge_w)
    y_s = _peer_layer(h_s, peer_w).reshape(db, 1, D_MODEL)

    def rows(d, bb, tt_):
        return (d["ckv"].reshape(1, bb, tt_, KV_LORA), d["kr"].reshape(1, bb, tt_, ROPE_DIM),
                d["kb"].reshape(1, bb, tt_, KV_HEADS_B, HD_B), d["vb"].reshape(1, bb, tt_, KV_HEADS_B, HD_B),
                d["ki"].reshape(1, bb, tt_, IDX_DIM))

    return (y_p, y_s) + rows(pp_, b, t) + rows(sp, db, 1)
```

```python
import functools
import math

import numpy as np
import jax
import jax.numpy as jnp
from jax import lax
from jax.experimental import pallas as pl
from jax.experimental.pallas import tpu as pltpu

F32 = jnp.float32
BF16 = jnp.bfloat16
I32 = jnp.int32

D_MODEL = 2048
PAGE_SIZE = 128
H_A = 8
Q_LORA = 512
KV_LORA = 512
NOPE_DIM = 128
ROPE_DIM = 64
V_DIM = 128
MLA_SCALE = (NOPE_DIM + ROPE_DIM) ** -0.5
H_B = 8
HD_B = 128
KV_HEADS_B = 2
ROT_B = HD_B // 4
DSA_SCALE = HD_B ** -0.5
H_IDX = 16
IDX_DIM = 64
ROT_IDX = IDX_DIM // 4
IDX_SCALE = IDX_DIM ** -0.5
W_IDX_SCALE = H_IDX ** -0.5
TOPK_MAX = 256
PEER_HEADS = 8
N_KEYS = 128
PEER_DQ = 256
PEER_TOPK = 16
ROPE_THETA = 500000.0
RMS_EPS = 1e-6
LN_EPS = 1e-5
DEPTH = 1
ALPHA = (2 * DEPTH) ** 0.25

IN_SIZES = (Q_LORA, KV_LORA + ROPE_DIM, H_B * HD_B, KV_HEADS_B * HD_B, KV_HEADS_B * HD_B,
            H_IDX * IDX_DIM, IDX_DIM, H_IDX, 2 * D_MODEL)
IN_OFFSETS = tuple(int(o) for o in np.cumsum(IN_SIZES)[:-1])

LANE = 128
INT_MIN = -2 ** 31
VMEM_LIMIT = 56 * 1024 * 1024

O_Q = 0
O_C = O_Q + Q_LORA
O_KR = O_C + KV_LORA
O_KB = O_KR + LANE
O_VB = O_KB + KV_HEADS_B * HD_B
O_KI = O_VB + KV_HEADS_B * HD_B
O_WI = O_KI + LANE
P1_COLS = O_WI + LANE

NT_DIMS = (((1,), (1,)), ((), ()))


def _cparams(sem):
    return pltpu.CompilerParams(dimension_semantics=sem, vmem_limit_bytes=VMEM_LIMIT)


def _const_spec(shape):
    n = len(shape)
    return pl.BlockSpec(shape, lambda *_: (0,) * n)


def _rms(x, g):
    return x * lax.rsqrt(jnp.mean(x * x, axis=-1, keepdims=True) + RMS_EPS) * g


def _layer_norm(x, g, b):
    mu = jnp.mean(x, axis=-1, keepdims=True)
    d = x - mu
    var = jnp.mean(d * d, axis=-1, keepdims=True)
    return d * lax.rsqrt(var + LN_EPS) * g + b


def _rope(x, tab_ref, half):
    w = x.shape[-1]
    return (x * tab_ref[0] + pltpu.roll(x, w - half, 1) * tab_ref[1] + pltpu.roll(x, half, 1) * tab_ref[2])


def _rope_table(pos, width, head_dim, rot):
    half = rot // 2
    inv = jnp.power(ROPE_THETA, -jnp.arange(half, dtype=F32) / half)
    ang = pos.astype(F32)[:, None] * inv[None, :]
    cos, sin = jnp.cos(ang), jnp.sin(ang)
    t = pos.shape[0]
    rest = jnp.zeros((t, head_dim - rot), F32)
    zh = jnp.zeros((t, half), F32)
    c = jnp.concatenate([cos, cos, rest + 1.0], axis=1)
    s1 = jnp.concatenate([-sin, zh, rest], axis=1)
    s2 = jnp.concatenate([zh, sin, rest], axis=1)
    return jnp.stack([jnp.tile(a, (1, width // head_dim)) for a in (c, s1, s2)])


def _sort_key(s):
    bits = pltpu.bitcast(s + 0.0, I32)
    return jnp.where(bits < 0, bits ^ 0x7FFFFFFF, bits)


def _kth_largest_key(key, k):
    kf = float(k)

    def count_ge(c):
        return jnp.sum((key >= c).astype(F32), axis=1, keepdims=True)

    ans = jnp.where(count_ge(jnp.zeros_like(key[:, :1])) >= kf, 0, INT_MIN).astype(I32)

    def body(i, ans):
        cand = ans | jnp.left_shift(jnp.int32(1), 30 - i)
        return jnp.where(count_ge(cand) >= kf, cand, ans)

    return lax.fori_loop(0, 31, body, ans)


def _top_rows(vals, k):
    r = vals.shape[0]
    rid = lax.broadcasted_iota(I32, vals.shape, 0).astype(F32)
    tops, ids = [], []
    for _ in range(k):
        m = jnp.max(vals, axis=0, keepdims=True)
        idx = jnp.min(jnp.where(vals == m, rid, float(r)), axis=0, keepdims=True)
        vals = jnp.where(rid == idx, -jnp.inf, vals)
        tops.append(m)
        ids.append(idx)
    return jnp.concatenate(tops, axis=0), jnp.concatenate(ids, axis=0)


def _proj1_kernel(x_ref, w_ref, qg_ref, kvg_ref, ig_ref, ib_ref, tkr_ref, tkb_ref, tki_ref,
                  qn_ref, ckv_ref, kr_ref, kb_ref, vb_ref, ki_ref, wi_ref,
                  ckvb_ref, krb_ref, kbb_ref, vbb_ref, kib_ref):
    z = jnp.dot(x_ref[...].astype(BF16), w_ref[...], preferred_element_type=F32)
    qn_ref[...] = _rms(z[:, O_Q:O_Q + Q_LORA], qg_ref[...]).astype(BF16)
    ckv = _rms(z[:, O_C:O_C + KV_LORA], kvg_ref[...])
    ckv_ref[...] = ckv
    ckvb_ref[...] = ckv.astype(BF16)
    kr = _rope(z[:, O_KR:O_KR + LANE], tkr_ref, ROPE_DIM // 2)[:, :ROPE_DIM]
    kr_ref[...] = kr
    krb_ref[...] = kr.astype(BF16)
    kb = _rope(z[:, O_KB:O_KB + KV_HEADS_B * HD_B], tkb_ref, ROT_B // 2)
    kb_ref[...] = kb
    kbb_ref[...] = kb.astype(BF16)
    vb = z[:, O_VB:O_VB + KV_HEADS_B * HD_B]
    vb_ref[...] = vb
    vbb_ref[...] = vb.astype(BF16)
    zi = z[:, O_KI:O_KI + LANE]
    real = lax.broadcasted_iota(I32, zi.shape, 1) < IDX_DIM
    mu = jnp.sum(zi, axis=-1, keepdims=True) * (1.0 / IDX_DIM)
    d = jnp.where(real, zi - mu, 0.0)
    var = jnp.sum(d * d, axis=-1, keepdims=True) * (1.0 / IDX_DIM)
    ki = _rope(d * lax.rsqrt(var + LN_EPS) * ig_ref[...] + ib_ref[...], tki_ref, ROT_IDX // 2)[:, :IDX_DIM]
    ki_ref[...] = ki
    kib_ref[...] = ki.astype(BF16)
    wi_ref[...] = z[:, O_WI:O_WI + H_IDX] * W_IDX_SCALE


def _proj2_kernel(x_ref, w_ref, tqb_ref, tqi_ref, qb_ref, qi_ref):
    z = jnp.dot(x_ref[...].astype(BF16), w_ref[...], preferred_element_type=F32)
    nb = H_B * HD_B
    qb_ref[...] = _rope(z[:, :nb], tqb_ref, ROT_B // 2).astype(BF16)
    qi_ref[...] = (_rope(z[:, nb:], tqi_ref, ROT_IDX // 2) * IDX_SCALE).astype(BF16)


def _gate_kernel(x_ref, w_ref, g_ref):
    z = jnp.dot(x_ref[...].astype(BF16), w_ref[...], preferred_element_type=F32)
    g_ref[...] = jax.nn.sigmoid(z)


def _qpath_kernel(qn_ref, wuq_ref, wukt_ref, tqr_ref, ql_ref, qr_ref):
    qa = jnp.dot(qn_ref[...], wuq_ref[...], preferred_element_type=F32)
    n_nope = H_A * NOPE_DIM
    qr = _rope(qa[:, n_nope:], tqr_ref, ROPE_DIM // 2).astype(BF16)
    for h in range(H_A):
        qr_ref[0, h] = qr[:, h * ROPE_DIM:(h + 1) * ROPE_DIM]
        nope = qa[:, h * NOPE_DIM:(h + 1) * NOPE_DIM].astype(BF16)
        ql_ref[0, h] = jnp.dot(nope, wukt_ref[h], preferred_element_type=F32).astype(BF16)


def _tab_spec(tab, tm, tiles_per_batch):
    w = tab.shape[-1]
    if tab.shape[1] == 1:
        return pl.BlockSpec((3, 1, w), lambda i: (0, 0, 0))
    return pl.BlockSpec((3, tm, w), lambda i: (0, i % tiles_per_batch, 0))


def _project(x2, pos, n_batch, weights):
    n = x2.shape[0]
    t = pos.shape[0]
    tm = min(256, n if t == 1 else t)
    tpb = max(1, t // tm)
    w1, w2, wg, qg, kvg, ig, ib, wuq, wukt = weights
    tkr = _rope_table(pos, LANE, LANE, ROPE_DIM)
    tkb = _rope_table(pos, KV_HEADS_B * HD_B, HD_B, ROT_B)
    tki = _rope_table(pos, LANE, LANE, ROT_IDX)
    tqb = _rope_table(pos, H_B * HD_B, HD_B, ROT_B)
    tqi = _rope_table(pos, H_IDX * IDX_DIM, IDX_DIM, ROT_IDX)
    tqr = _rope_table(pos, H_A * ROPE_DIM, ROPE_DIM, ROPE_DIM)
    row = lambda w: pl.BlockSpec((tm, w), lambda i: (i, 0))
    xspec = row(D_MODEL)
    kvw = KV_HEADS_B * HD_B
    outs1 = pl.pallas_call(
        _proj1_kernel,
        grid=(n // tm,),
        in_specs=[xspec, _const_spec((D_MODEL, P1_COLS)), _const_spec((1, Q_LORA)), _const_spec((1, KV_LORA)),
                  _const_spec((1, LANE)), _const_spec((1, LANE)),
                  _tab_spec(tkr, tm, tpb), _tab_spec(tkb, tm, tpb), _tab_spec(tki, tm, tpb)],
        out_specs=[row(Q_LORA), row(KV_LORA), row(ROPE_DIM), row(kvw), row(kvw), row(IDX_DIM), row(H_IDX),
                   row(KV_LORA), row(ROPE_DIM), row(kvw), row(kvw), row(IDX_DIM)],
        out_shape=[jax.ShapeDtypeStruct((n, Q_LORA), BF16), jax.ShapeDtypeStruct((n, KV_LORA), F32),
                   jax.ShapeDtypeStruct((n, ROPE_DIM), F32), jax.ShapeDtypeStruct((n, kvw), F32),
                   jax.ShapeDtypeStruct((n, kvw), F32), jax.ShapeDtypeStruct((n, IDX_DIM), F32),
                   jax.ShapeDtypeStruct((n, H_IDX), F32),
                   jax.ShapeDtypeStruct((n, KV_LORA), BF16), jax.ShapeDtypeStruct((n, ROPE_DIM), BF16),
                   jax.ShapeDtypeStruct((n, kvw), BF16), jax.ShapeDtypeStruct((n, kvw), BF16),
                   jax.ShapeDtypeStruct((n, IDX_DIM), BF16)],
        compiler_params=_cparams(("parallel",)),
        name="proj1",
    )(x2, w1, qg, kvg, ig, ib, tkr, tkb, tki)
    qn, ckv, kr, kb, vb, ki, wi, ckv_b, kr_b, kb_b, vb_b, ki_b = outs1

    qb, qi = pl.pallas_call(
        _proj2_kernel,
        grid=(n // tm,),
        in_specs=[xspec, _const_spec(w2.shape), _tab_spec(tqb, tm, tpb), _tab_spec(tqi, tm, tpb)],
        out_specs=[row(H_B * HD_B), row(H_IDX * IDX_DIM)],
        out_shape=[jax.ShapeDtypeStruct((n, H_B * HD_B), BF16), jax.ShapeDtypeStruct((n, H_IDX * IDX_DIM), BF16)],
        compiler_params=_cparams(("parallel",)),
        name="proj2",
    )(x2, w2, tqb, tqi)

    tn = D_MODEL
    tg = min(512, n)
    gates = pl.pallas_call(
        _gate_kernel,
        grid=(2 * D_MODEL // tn, n // tg),
        in_specs=[pl.BlockSpec((tg, D_MODEL), lambda j, i: (i, 0)), pl.BlockSpec((D_MODEL, tn), lambda j, i: (0, j))],
        out_specs=pl.BlockSpec((tg, tn), lambda j, i: (i, j)),
        out_shape=jax.ShapeDtypeStruct((n, 2 * D_MODEL), F32),
        compiler_params=_cparams(("parallel", "parallel")),
        name="gates",
    )(x2, wg)

    t_eff = n // n_batch
    ql, qr = pl.pallas_call(
        _qpath_kernel,
        grid=(n // tm,),
        in_specs=[row(Q_LORA), _const_spec(wuq.shape), _const_spec(wukt.shape), _tab_spec(tqr, tm, tpb)],
        out_specs=[pl.BlockSpec((1, H_A, tm, KV_LORA), lambda i: (i // (t_eff // tm), 0, i % (t_eff // tm), 0)),
                   pl.BlockSpec((1, H_A, tm, ROPE_DIM), lambda i: (i // (t_eff // tm), 0, i % (t_eff // tm), 0))],
        out_shape=[jax.ShapeDtypeStruct((n_batch, H_A, t_eff, KV_LORA), BF16),
                   jax.ShapeDtypeStruct((n_batch, H_A, t_eff, ROPE_DIM), BF16)],
        compiler_params=_cparams(("parallel",)),
        name="qpath",
    )(qn, wuq, wukt, tqr)
    return dict(ckv=ckv, kr=kr, kb=kb, vb=vb, ki=ki, wi=wi, ckv_b=ckv_b, kr_b=kr_b, kb_b=kb_b, vb_b=vb_b,
                ki_b=ki_b, qb=qb, qi=qi, gates=gates, ql=ql, qr=qr)


def _mla_prompt_kernel(ql_ref, qr_ref, c_ref, kr_ref, o_ref, m_sc, l_sc, acc_sc, *, tq, tk):
    i, j = pl.program_id(1), pl.program_id(2)
    rows = H_A * tq

    @pl.when(j == 0)
    def _():
        m_sc[...] = jnp.full_like(m_sc, -jnp.inf)
        l_sc[...] = jnp.zeros_like(l_sc)
        acc_sc[...] = jnp.zeros_like(acc_sc)

    @pl.when(j * tk < (i + 1) * tq)
    def _():
        ql = ql_ref[0].reshape(rows, KV_LORA)
        qr = qr_ref[0].reshape(rows, ROPE_DIM)
        c = c_ref[0]
        s = (lax.dot_general(ql, c, NT_DIMS, preferred_element_type=F32)
             + lax.dot_general(qr, kr_ref[0], NT_DIMS, preferred_element_type=F32)) * MLA_SCALE
        q_pos = i * tq + lax.broadcasted_iota(I32, (H_A, tq, tk), 1).reshape(rows, tk)
        k_pos = j * tk + lax.broadcasted_iota(I32, (rows, tk), 1)
        s = jnp.where(k_pos <= q_pos, s, -jnp.inf)
        m_new = jnp.maximum(m_sc[...], jnp.max(s, axis=1, keepdims=True))
        a = jnp.exp(m_sc[...] - m_new)
        p = jnp.exp(s - m_new)
        l_sc[...] = a * l_sc[...] + jnp.sum(p, axis=1, keepdims=True)
        acc_sc[...] = a * acc_sc[...] + jnp.dot(p.astype(BF16), c, preferred_element_type=F32)
        m_sc[...] = m_new

    @pl.when(j == pl.num_programs(2) - 1)
    def _():
        o_ref[0] = (acc_sc[...] / l_sc[...]).astype(BF16).reshape(H_A, tq, KV_LORA)


def _mla_prompt(ql, qr, ckv_b, kr_b):
    b, _, t, _ = ql.shape
    tq = min(256, t)
    tk = min(512, t)
    last = lambda i, j: jnp.minimum(j, ((i + 1) * tq - 1) // tk)
    return pl.pallas_call(
        functools.partial(_mla_prompt_kernel, tq=tq, tk=tk),
        grid=(b, t // tq, t // tk),
        in_specs=[pl.BlockSpec((1, H_A, tq, KV_LORA), lambda bb, i, j: (bb, 0, i, 0)),
                  pl.BlockSpec((1, H_A, tq, ROPE_DIM), lambda bb, i, j: (bb, 0, i, 0)),
                  pl.BlockSpec((1, tk, KV_LORA), lambda bb, i, j: (bb, last(i, j), 0)),
                  pl.BlockSpec((1, tk, ROPE_DIM), lambda bb, i, j: (bb, last(i, j), 0))],
        out_specs=pl.BlockSpec((1, H_A, tq, KV_LORA), lambda bb, i, j: (bb, 0, i, 0)),
        out_shape=jax.ShapeDtypeStruct((b, H_A, t, KV_LORA), BF16),
        scratch_shapes=[pltpu.VMEM((H_A * tq, 1), F32), pltpu.VMEM((H_A * tq, 1), F32),
                        pltpu.VMEM((H_A * tq, KV_LORA), F32)],
        compiler_params=_cparams(("parallel", "parallel", "arbitrary")),
        name="mla_prompt",
    )(ql, qr, ckv_b, kr_b)


def _dsa_prompt_kernel(qi_ref, wi_ref, qb_ref, ki_ref, kb_ref, vb_ref, o_ref, *, tq, k_sel):
    i = pl.program_id(1)
    t_all = ki_ref.shape[1]
    ki = ki_ref[0]
    wi = wi_ref[...]
    score = jnp.zeros((tq, t_all), F32)
    for h in range(H_IDX):
        d = lax.dot_general(qi_ref[:, h * IDX_DIM:(h + 1) * IDX_DIM], ki, NT_DIMS, preferred_element_type=F32)
        score = score + wi[:, h:h + 1] * jnp.maximum(d, 0.0)
    q_pos = i * tq + lax.broadcasted_iota(I32, (tq, t_all), 0)
    k_pos = lax.broadcasted_iota(I32, (tq, t_all), 1)
    causal = k_pos <= q_pos
    key = jnp.where(causal, _sort_key(score), INT_MIN)
    thr = _kth_largest_key(key, k_sel)
    keep = (causal & (key >= thr))[None]
    hpg = H_B // KV_HEADS_B
    for g in range(KV_HEADS_B):
        qg = jnp.concatenate([qb_ref[:, (g * hpg + jj) * HD_B:(g * hpg + jj + 1) * HD_B] for jj in range(hpg)], axis=0)
        kg = kb_ref[0, :, g * HD_B:(g + 1) * HD_B]
        vg = vb_ref[0, :, g * HD_B:(g + 1) * HD_B]
        s = lax.dot_general(qg, kg, NT_DIMS, preferred_element_type=F32) * DSA_SCALE
        s = jnp.where(keep, s.reshape(hpg, tq, t_all), -jnp.inf).reshape(hpg * tq, t_all)
        p = jnp.exp(s - jnp.max(s, axis=1, keepdims=True))
        l = jnp.sum(p, axis=1, keepdims=True)
        o = (jnp.dot(p.astype(BF16), vg, preferred_element_type=F32) / l).astype(BF16)
        for jj in range(hpg):
            o_ref[:, (g * hpg + jj) * HD_B:(g * hpg + jj + 1) * HD_B] = o[jj * tq:(jj + 1) * tq]


def _dsa_prompt(qi, wi, qb, ki_b, kb_b, vb_b, b, t):
    tq = min(128, t)
    k_sel = min(TOPK_MAX, t // 4)
    tpb = t // tq
    row = lambda w: pl.BlockSpec((tq, w), lambda bb, i: (bb * tpb + i, 0))
    per_b = lambda w: pl.BlockSpec((1, t, w), lambda bb, i: (bb, 0, 0))
    kvw = KV_HEADS_B * HD_B
    return pl.pallas_call(
        functools.partial(_dsa_prompt_kernel, tq=tq, k_sel=k_sel),
        grid=(b, tpb),
        in_specs=[row(H_IDX * IDX_DIM), row(H_IDX), row(H_B * HD_B), per_b(IDX_DIM), per_b(kvw), per_b(kvw)],
        out_specs=row(H_B * HD_B),
        out_shape=jax.ShapeDtypeStruct((b * t, H_B * HD_B), BF16),
        compiler_params=_cparams(("parallel", "arbitrary")),
        name="dsa_prompt",
    )(qi, wi, qb, ki_b.reshape(b, t, IDX_DIM), kb_b.reshape(b, t, kvw), vb_b.reshape(b, t, kvw))


def _merge_kernel(ol_ref, ob_ref, g_ref, x_ref, wuv_ref, wao_ref, wbo_ref, wout_ref, lg_ref, lb_ref, h_ref):
    ta = jnp.concatenate([jnp.dot(ol_ref[0, h], wuv_ref[h], preferred_element_type=F32) for h in range(H_A)],
                         axis=1).astype(BF16)
    ya = jnp.dot(ta, wao_ref[...], preferred_element_type=F32)
    yb = jnp.dot(ob_ref[...], wbo_ref[...], preferred_element_type=F32)
    m = (g_ref[:, :D_MODEL] * ya + g_ref[:, D_MODEL:] * yb).astype(BF16)
    mix = jnp.dot(m, wout_ref[...], preferred_element_type=F32)
    h_ref[...] = _layer_norm(ALPHA * x_ref[...] + mix, lg_ref[...], lb_ref[...])


def _merge(o_lat, o_b, gates, x2, weights):
    b, _, t, _ = o_lat.shape
    tm = min(256, t)
    tpb = t // tm
    wuv, wao, wbo, wout, lg, lb = weights
    row = lambda w: pl.BlockSpec((tm, w), lambda i: (i, 0))
    once = lambda a: pl.BlockSpec(a.shape, lambda i: (0,) * a.ndim, pipeline_mode=pl.Buffered(1))
    return pl.pallas_call(
        _merge_kernel,
        grid=(b * tpb,),
        in_specs=[pl.BlockSpec((1, H_A, tm, KV_LORA), lambda i: (i // tpb, 0, i % tpb, 0)),
                  row(H_B * HD_B), row(2 * D_MODEL), row(D_MODEL),
                  once(wuv), once(wao), once(wbo), once(wout), once(lg), once(lb)],
        out_specs=row(D_MODEL),
        out_shape=jax.ShapeDtypeStruct((b * t, D_MODEL), F32),
        compiler_params=_cparams(("parallel",)),
        name="merge",
    )(o_lat, o_b, gates, x2, wuv, wao, wbo, wout, lg, lb)


def _peer_route_kernel(h_ref, wq_ref, sk_ref, e_ref, g_ref):
    q = jnp.dot(h_ref[...].astype(BF16), wq_ref[...], preferred_element_type=F32).astype(BF16)
    half = PEER_DQ // 2
    gates, experts = [], []
    for h in range(PEER_HEADS):
        sv, si = [], []
        for c in range(2):
            qhc = q[:, (2 * h + c) * half:(2 * h + c + 1) * half]
            s_t = lax.dot_general(sk_ref[h, c], qhc, NT_DIMS, preferred_element_type=F32)
            v, ix = _top_rows(s_t, PEER_TOPK)
            sv.append(v)
            si.append(ix)
        kk = PEER_TOPK
        tm = sv[0].shape[1]
        a8 = lax.broadcasted_iota(I32, (8, tm), 0)
        a16 = lax.broadcasted_iota(I32, (kk, tm), 0)
        cand = [sv[0] + sv[1][0:1]]
        cidx = [si[0] * float(N_KEYS) + si[1][0:1]]
        flat = [a16 * kk]
        for b in range(1, 8):
            ok = (a8 + 1) * (b + 1) <= kk
            cand.append(jnp.where(ok, sv[0][0:8] + sv[1][b:b + 1], -jnp.inf))
            cidx.append(si[0][0:8] * float(N_KEYS) + si[1][b:b + 1])
            flat.append(a8 * kk + b)
        cand.append(sv[0][0:1] + sv[1][8:kk])
        cidx.append(si[0][0:1] * float(N_KEYS) + si[1][8:kk])
        flat.append(a8 + 8)
        cand = jnp.concatenate(cand, axis=0)
        cidx = jnp.concatenate(cidx, axis=0)
        rid = jnp.concatenate(flat, axis=0).astype(F32)
        fv, fe = [], []
        for _ in range(PEER_TOPK):
            m = jnp.max(cand, axis=0, keepdims=True)
            pos = jnp.min(jnp.where(cand == m, rid, float(kk * kk)), axis=0, keepdims=True)
            hit = rid == pos
            fe.append(jnp.sum(jnp.where(hit, cidx, 0.0), axis=0, keepdims=True))
            cand = jnp.where(hit, -jnp.inf, cand)
            fv.append(m)
        fv = jnp.concatenate(fv, axis=0)
        p = jnp.exp(fv - fv[0:1])
        gates.append(p / jnp.sum(p, axis=0, keepdims=True))
        experts.append(jnp.concatenate(fe, axis=0))
    g_ref[...] = jnp.concatenate(gates, axis=0).T
    e_ref[...] = jnp.concatenate(experts, axis=0).T.astype(I32)


def _peer_route(h2, wq, sk):
    n = h2.shape[0]
    tm = min(128, n)
    once = lambda a: pl.BlockSpec(a.shape, lambda i: (0,) * a.ndim)
    return pl.pallas_call(
        _peer_route_kernel,
        grid=(n // tm,),
        in_specs=[pl.BlockSpec((tm, D_MODEL), lambda i: (i, 0)), once(wq), once(sk)],
        out_specs=[pl.BlockSpec((tm, N_SEL), lambda i: (i, 0)), pl.BlockSpec((tm, N_SEL), lambda i: (i, 0))],
        out_shape=[jax.ShapeDtypeStruct((n, N_SEL), I32), jax.ShapeDtypeStruct((n, N_SEL), F32)],
        compiler_params=_cparams(("parallel",)),
        name="peer_route",
    )(h2, wq, sk)


N_SEL = PEER_HEADS * PEER_TOPK
PEER_TT = 32
PEER_SLOTS = 4
PEER_AHEAD = PEER_SLOTS - 1
IDX_RING = 3
PEER_CHUNK = 16
HALF_D = D_MODEL // 2


def _pack_bf16_pairs(a):
    b = lax.bitcast_convert_type(a.astype(BF16), jnp.uint16).astype(jnp.uint32)
    h = a.shape[1] // 2
    return lax.bitcast_convert_type(b[:, :h] | (b[:, h:] << 16), I32)


def _low_bf16(w):
    return pltpu.bitcast(w << 16, F32)


def _high_bf16(w):
    return pltpu.bitcast(w & -65536, F32)


def _peer_expert_kernel(e_hbm, h_ref, g_ref, tab_hbm, lg_ref, lb_ref, y_ref,
                        idx_sm, *rest, n_total, n_steps):
    buf, (po_sc, sem_i, sem_g) = rest[:PEER_SLOTS], rest[PEER_SLOTS:]
    i = pl.program_id(0)
    tt = PEER_TT

    def idx_copy(step):
        ring = step % IDX_RING
        return pltpu.make_async_copy(e_hbm.at[step], idx_sm.at[pl.ds(ring * tt, tt)], sem_i.at[ring])

    def row_copy(e, slot, k):
        return pltpu.make_async_copy(tab_hbm.at[e], buf[slot].at[pl.ds(k, 1)], sem_g.at[slot])

    def idx_row(n):
        n = jnp.minimum(n, n_total - 1)
        return ((n // tt) % IDX_RING) * tt + n % tt

    def issue(row, slot, k0, k1):
        for k in range(k0, k1):
            row_copy(idx_sm[row, k], slot, k).start()

    def wait(slot):
        other = buf[(slot + 1) % PEER_SLOTS]
        pltpu.make_async_copy(other, buf[slot], sem_g.at[slot]).wait()

    @pl.when(i == 0)
    def _():
        idx_copy(0).start()
        if n_steps > 1:
            idx_copy(1).start()
        idx_copy(0).wait()
        for t in range(PEER_AHEAD):
            issue(idx_row(t), t, 0, N_SEL)

    @pl.when(i + 1 < n_steps)
    def _():
        idx_copy(i + 1).wait()

    @pl.when(i + 2 < n_steps)
    def _():
        idx_copy(i + 2).start()

    n_chunks = N_SEL // PEER_CHUNK
    per_chunk = N_SEL // (2 * n_chunks)

    def token(t, slot):
        wait(slot)
        row = idx_row(i * tt + t + PEER_AHEAD)
        ahead_slot = (slot + PEER_AHEAD) % PEER_SLOTS
        x = h_ref[pl.ds(t, 1), :]
        x_lo, x_hi = x[:, :HALF_D], x[:, HALF_D:]
        words = buf[slot]
        hk = []
        for c in range(n_chunks):
            wu = words[c * PEER_CHUNK:(c + 1) * PEER_CHUNK, :HALF_D]
            hk.append(jnp.sum(_low_bf16(wu) * x_lo + _high_bf16(wu) * x_hi, axis=1, keepdims=True))
            issue(row, ahead_slot, c * per_chunk, (c + 1) * per_chunk)
        hk = jnp.concatenate(hk, axis=0)
        gcol = jnp.transpose(jnp.broadcast_to(g_ref[pl.ds(t, 1), :], (N_SEL, N_SEL)))[:, 0:1]
        a = gcol * (0.5 * hk * (1.0 + lax.erf(hk * math.sqrt(0.5))))
        acc_lo = jnp.zeros((8, HALF_D), F32)
        acc_hi = jnp.zeros((8, HALF_D), F32)
        for c in range(n_chunks):
            wv = words[c * PEER_CHUNK:(c + 1) * PEER_CHUNK, HALF_D:]
            ac = a[c * PEER_CHUNK:(c + 1) * PEER_CHUNK]
            for r in range(0, PEER_CHUNK, 8):
                acc_lo = acc_lo + _low_bf16(wv[r:r + 8]) * ac[r:r + 8]
                acc_hi = acc_hi + _high_bf16(wv[r:r + 8]) * ac[r:r + 8]
            issue(row, ahead_slot, (n_chunks + c) * per_chunk, (n_chunks + c + 1) * per_chunk)
        po_sc[pl.ds(t, 1), :] = jnp.concatenate(
            [jnp.sum(acc_lo, axis=0, keepdims=True), jnp.sum(acc_hi, axis=0, keepdims=True)], axis=1)

    def group(g, carry):
        for j in range(PEER_SLOTS):
            token(g * PEER_SLOTS + j, j)
        return carry

    lax.fori_loop(0, tt // PEER_SLOTS, group, 0)

    @pl.when(i == n_steps - 1)
    def _():
        for j in range(PEER_AHEAD):
            wait(j)

    y_ref[...] = _layer_norm(ALPHA * h_ref[...] + po_sc[...], lg_ref[...], lb_ref[...])


def _peer_experts(eidx, gate, h2, table, lg, lb):
    n = h2.shape[0]
    tt = PEER_TT
    assert n % tt == 0 and tt % PEER_SLOTS == 0
    steps = n // tt
    row = lambda w: pl.BlockSpec((tt, w), lambda i: (i, 0))
    anyspec = pl.BlockSpec(memory_space=pl.ANY)
    return pl.pallas_call(
        functools.partial(_peer_expert_kernel, n_total=n, n_steps=steps),
        grid=(steps,),
        in_specs=[anyspec, row(D_MODEL), row(N_SEL), anyspec,
                  _const_spec((1, D_MODEL)), _const_spec((1, D_MODEL))],
        out_specs=row(D_MODEL),
        out_shape=jax.ShapeDtypeStruct((n, D_MODEL), F32),
        scratch_shapes=[pltpu.SMEM((IDX_RING * tt, N_SEL), I32),
                        *[pltpu.VMEM((N_SEL, D_MODEL), I32) for _ in range(PEER_SLOTS)],
                        pltpu.VMEM((tt, D_MODEL), F32),
                        pltpu.SemaphoreType.DMA((IDX_RING,)),
                        pltpu.SemaphoreType.DMA((PEER_SLOTS,))],
        compiler_params=_cparams(("arbitrary",)),
        name="peer_experts",
    )(eidx.reshape(steps, tt, N_SEL), h2, gate, table, lg, lb)


def _peer_layer(h2, peer_w):
    wq, sk, table, lg, lb = peer_w
    eidx, gate = _peer_route(h2, wq, sk)
    return _peer_experts(eidx, gate, h2, table, lg, lb)


PAGES_PER_STEP = 16


def _sample_pass1_kernel(pt_ref, ql_ref, qr_ref, qi_ref, wi_ref, cn_ref, krn_ref, *rest):
    pp = PAGES_PER_STEP
    c_refs, kr_refs, ki_refs = rest[:pp], rest[pp:2 * pp], rest[2 * pp:3 * pp]
    o_ref, sc_ref, m_sc, l_sc, acc_sc = rest[3 * pp:]
    p_id = pl.program_id(1)

    @pl.when(p_id == 0)
    def _():
        m_sc[...] = jnp.full_like(m_sc, -jnp.inf)
        l_sc[...] = jnp.zeros_like(l_sc)
        acc_sc[...] = jnp.zeros_like(acc_sc)

    ql, qr = ql_ref[0], qr_ref[0]

    def absorb(s, weighted_values):
        s = s * MLA_SCALE
        m_new = jnp.maximum(m_sc[...], jnp.max(s, axis=1, keepdims=True))
        a = jnp.exp(m_sc[...] - m_new)
        p = jnp.exp(s - m_new)
        l_sc[...] = a * l_sc[...] + jnp.sum(p, axis=1, keepdims=True)
        acc_sc[...] = a * acc_sc[...] + weighted_values(p.astype(BF16))
        m_sc[...] = m_new

    c = jnp.concatenate([r[0].astype(BF16) for r in c_refs], axis=0)
    kr_t = jnp.concatenate([r[0].astype(BF16) for r in kr_refs], axis=1)
    absorb(lax.dot_general(ql, c, NT_DIMS, preferred_element_type=F32)
           + jnp.dot(qr, kr_t, preferred_element_type=F32),
           lambda p: jnp.dot(p, c, preferred_element_type=F32))
    ki_t = jnp.concatenate([r[0].astype(BF16) for r in ki_refs], axis=1)
    d = jnp.dot(qi_ref[0], ki_t, preferred_element_type=F32)
    sc_ref[0] = jnp.sum(wi_ref[0] * jnp.maximum(d, 0.0), axis=0, keepdims=True)

    @pl.when(p_id == pl.num_programs(1) - 1)
    def _():
        cn = cn_ref[0].astype(F32)
        s_new = (jnp.sum(ql.astype(F32) * cn, axis=1, keepdims=True)
                 + jnp.sum(qr.astype(F32) * krn_ref[0].astype(F32), axis=1, keepdims=True))
        absorb(s_new, lambda p: p.astype(F32) * cn)
        o_ref[0] = (acc_sc[...] / l_sc[...]).astype(BF16)


def _sample_pass1(page_table, ql, qr, qi3, wi3, ckv_new, kr_new, c_pool, kr_pool, ki_pool):
    db, n_pages = page_table.shape
    pp = PAGES_PER_STEP
    per_b = lambda a: pl.BlockSpec((1,) + a.shape[1:], lambda b, p, pt: (b,) + (0,) * (a.ndim - 1))

    def page_specs(*page_shape):
        return [pl.BlockSpec((1,) + page_shape, lambda b, p, pt, j=j: (pt[b, p * pp + j], 0, 0))
                for j in range(pp)]

    grid_spec = pltpu.PrefetchScalarGridSpec(
        num_scalar_prefetch=1,
        grid=(db, n_pages // pp),
        in_specs=[per_b(ql), per_b(qr), per_b(qi3), per_b(wi3), per_b(ckv_new), per_b(kr_new)]
                 + page_specs(PAGE_SIZE, KV_LORA) + page_specs(ROPE_DIM, PAGE_SIZE) + page_specs(IDX_DIM, PAGE_SIZE),
        out_specs=[pl.BlockSpec((1, H_A, KV_LORA), lambda b, p, pt: (b, 0, 0)),
                   pl.BlockSpec((1, 1, pp * PAGE_SIZE), lambda b, p, pt: (b, 0, p))],
        scratch_shapes=[pltpu.VMEM((H_A, 1), F32), pltpu.VMEM((H_A, 1), F32), pltpu.VMEM((H_A, KV_LORA), F32)],
    )
    return pl.pallas_call(
        _sample_pass1_kernel,
        grid_spec=grid_spec,
        out_shape=[jax.ShapeDtypeStruct((db, H_A, KV_LORA), BF16),
                   jax.ShapeDtypeStruct((db, 1, n_pages * PAGE_SIZE), F32)],
        compiler_params=_cparams(("parallel", "arbitrary")),
        name="sample_pass1",
    )(page_table, ql, qr, qi3, wi3, ckv_new, kr_new, *([c_pool] * pp), *([kr_pool] * pp), *([ki_pool] * pp))


def _sample_select_kernel(sc_ref, qi_ref, wi_ref, kin_ref, thr_ref, new_ref, *, k_sel):
    past = sc_ref[...]
    qi = qi_ref[...].astype(F32)
    kn = kin_ref[...].astype(F32)
    d = jnp.sum(qi * kn, axis=2)
    s_new = jnp.sum(wi_ref[...] * jnp.maximum(d, 0.0), axis=1, keepdims=True)
    key_new = _sort_key(s_new)
    lane0 = lax.broadcasted_iota(I32, (past.shape[0], LANE), 1) == 0
    key = jnp.concatenate([_sort_key(past), jnp.where(lane0, key_new, INT_MIN)], axis=1)
    thr = _kth_largest_key(key, k_sel)
    thr_ref[...] = thr
    new_ref[...] = (key_new >= thr).astype(I32)


def _sample_select(scores, qi3, wi2, ki_new3, k_sel):
    db, past = scores.shape
    rows = min(8, db)
    return pl.pallas_call(
        functools.partial(_sample_select_kernel, k_sel=k_sel),
        grid=(db // rows,),
        in_specs=[pl.BlockSpec((rows, past), lambda i: (i, 0)),
                  pl.BlockSpec((rows, H_IDX, IDX_DIM), lambda i: (i, 0, 0)),
                  pl.BlockSpec((rows, H_IDX), lambda i: (i, 0)),
                  pl.BlockSpec((rows, 1, IDX_DIM), lambda i: (i, 0, 0))],
        out_specs=[pl.BlockSpec((rows, 1), lambda i: (i, 0)), pl.BlockSpec((rows, 1), lambda i: (i, 0))],
        out_shape=[jax.ShapeDtypeStruct((db, 1), I32), jax.ShapeDtypeStruct((db, 1), I32)],
        compiler_params=_cparams(("parallel",)),
        name="sample_select",
    )(scores, qi3, wi2, ki_new3)


def _sample_pass2_kernel(pt_ref, thr_ref, new_ref, qb_ref, sc_ref, kn_ref, vn_ref, *rest):
    pp = PAGES_PER_STEP
    k_refs, v_refs = rest[:pp], rest[pp:2 * pp]
    o_ref, m_sc, l_sc, acc_sc = rest[2 * pp:]
    b, p_id = pl.program_id(0), pl.program_id(1)
    hpg = H_B // KV_HEADS_B

    @pl.when(p_id == 0)
    def _():
        m_sc[...] = jnp.full_like(m_sc, -jnp.inf)
        l_sc[...] = jnp.zeros_like(l_sc)
        acc_sc[...] = jnp.zeros_like(acc_sc)

    def absorb(g, s, keep, weighted_values):
        rows = slice(g * hpg, (g + 1) * hpg)
        s = jnp.where(keep, s * DSA_SCALE, -jnp.inf)
        m_old = m_sc[rows]
        m_new = jnp.maximum(m_old, jnp.max(s, axis=1, keepdims=True))
        m_safe = jnp.where(m_new == -jnp.inf, 0.0, m_new)
        a = jnp.exp(m_old - m_safe)
        p = jnp.exp(s - m_safe)
        l_sc[rows] = a * l_sc[rows] + jnp.sum(p, axis=1, keepdims=True)
        acc_sc[rows] = a * acc_sc[rows] + weighted_values(p.astype(BF16))
        m_sc[rows] = m_new

    keep = _sort_key(sc_ref[0]) >= thr_ref[b]
    for g in range(KV_HEADS_B):
        own = pl.ds(g, PAGE_SIZE, stride=KV_HEADS_B)
        k = jnp.concatenate([r[0, own, :].astype(BF16) for r in k_refs], axis=0)
        v = jnp.concatenate([r[0, own, :].astype(BF16) for r in v_refs], axis=0)
        absorb(g, lax.dot_general(qb_ref[0, g * hpg:(g + 1) * hpg], k, NT_DIMS, preferred_element_type=F32),
               keep, lambda p, v=v: jnp.dot(p, v, preferred_element_type=F32))

    @pl.when(p_id == pl.num_programs(1) - 1)
    def _():
        for g in range(KV_HEADS_B):
            cols = slice(g * HD_B, (g + 1) * HD_B)
            kn = kn_ref[0, :, cols].astype(F32)
            vn = vn_ref[0, :, cols].astype(F32)
            s_new = jnp.sum(qb_ref[0, g * hpg:(g + 1) * hpg].astype(F32) * kn, axis=1, keepdims=True)
            absorb(g, s_new, new_ref[b] > 0, lambda p, vn=vn: p.astype(F32) * vn)
        o_ref[0] = (acc_sc[...] / l_sc[...]).astype(BF16)


def _sample_pass2(page_table, thr, sel_new, qb3, scores3, kb_new, vb_new, k_pool, v_pool):
    db, n_pages = page_table.shape
    pp = PAGES_PER_STEP
    kvw = KV_HEADS_B * HD_B
    per_b = lambda a: pl.BlockSpec((1,) + a.shape[1:], lambda b, p, *_: (b,) + (0,) * (a.ndim - 1))
    pages = [pl.BlockSpec((1, PAGE_SIZE * KV_HEADS_B, HD_B), lambda b, p, pt, th, nw, j=j: (pt[b, p * pp + j], 0, 0))
             for j in range(pp)]
    grid_spec = pltpu.PrefetchScalarGridSpec(
        num_scalar_prefetch=3,
        grid=(db, n_pages // pp),
        in_specs=[per_b(qb3), pl.BlockSpec((1, 1, pp * PAGE_SIZE), lambda b, p, *_: (b, 0, p)),
                  per_b(kb_new), per_b(vb_new)] + pages + pages,
        out_specs=pl.BlockSpec((1, H_B, HD_B), lambda b, p, *_: (b, 0, 0)),
        scratch_shapes=[pltpu.VMEM((H_B, 1), F32), pltpu.VMEM((H_B, 1), F32), pltpu.VMEM((H_B, HD_B), F32)],
    )
    return pl.pallas_call(
        _sample_pass2_kernel,
        grid_spec=grid_spec,
        out_shape=jax.ShapeDtypeStruct((db, H_B, HD_B), BF16),
        compiler_params=_cparams(("parallel", "arbitrary")),
        name="sample_pass2",
    )(page_table, thr, sel_new, qb3, scores3, kb_new, vb_new, *([k_pool] * pp), *([v_pool] * pp))


def kernel(x_prompt, x_sample, cache_kv_latent, cache_k_rope, cache_k, cache_v, cache_idx_k, page_table, w_in, mla_q_norm, mla_w_uq, mla_kv_norm, mla_w_uk, mla_w_uv, mla_w_o, idx_k_norm_g, idx_k_norm_b, dsa_w_o, w_out, ln1_g, ln1_b, peer_w_q, peer_sub_keys, peer_u, peer_v, ln2_g, ln2_b):
    b, t, _ = x_prompt.shape
    db, ts, _ = x_sample.shape
    assert ts == 1, "the sample group decodes one token per sequence"
    n_pages = page_table.shape[1]
    past_len = n_pages * PAGE_SIZE
    l = 0

    wq_, wkv_, wbq_, wbk_, wbv_, wiq_, wik_, wiw_, wg_ = jnp.split(w_in[l], IN_OFFSETS, axis=1)
    padc = lambda a: jnp.pad(a, ((0, 0), (0, LANE - a.shape[1])))
    w1 = jnp.concatenate([wq_, wkv_[:, :KV_LORA], padc(wkv_[:, KV_LORA:]), wbk_, wbv_, padc(wik_), padc(wiw_)],
                         axis=1).astype(BF16)
    w2 = jnp.concatenate([wbq_, wiq_], axis=1).astype(BF16)
    wg = wg_.astype(BF16)
    wuq = jnp.concatenate([mla_w_uq[l][:, :, :NOPE_DIM].reshape(Q_LORA, H_A * NOPE_DIM),
                           mla_w_uq[l][:, :, NOPE_DIM:].reshape(Q_LORA, H_A * ROPE_DIM)], axis=1).astype(BF16)
    wukt = jnp.transpose(mla_w_uk[l], (1, 2, 0)).astype(BF16)
    row1 = lambda a: a.reshape(1, -1).astype(F32)
    pad1 = lambda a: jnp.pad(a.astype(F32), (0, LANE - a.shape[0])).reshape(1, LANE)
    proj_w = (w1, w2, wg, row1(mla_q_norm[l]), row1(mla_kv_norm[l]), pad1(idx_k_norm_g[l]), pad1(idx_k_norm_b[l]),
              wuq, wukt)
    merge_w = (jnp.transpose(mla_w_uv[l], (1, 0, 2)).astype(BF16), mla_w_o[l].astype(BF16),
               dsa_w_o[l].astype(BF16), w_out[l].astype(BF16), row1(ln1_g[l]), row1(ln1_b[l]))
    peer_table = jnp.concatenate([_pack_bf16_pairs(peer_u[l]), _pack_bf16_pairs(peer_v[l])], axis=1)
    peer_table = peer_table.reshape(peer_table.shape[0], 1, D_MODEL)
    peer_w = (peer_w_q[l].astype(BF16), peer_sub_keys[l].astype(BF16), peer_table, row1(ln2_g[l]), row1(ln2_b[l]))
    kvw = KV_HEADS_B * HD_B

    xp2 = x_prompt.reshape(b * t, D_MODEL)
    pp_ = _project(xp2, jnp.arange(t, dtype=I32), b, proj_w)
    o_lat_p = _mla_prompt(pp_["ql"], pp_["qr"], pp_["ckv_b"].reshape(b, t, KV_LORA), pp_["kr_b"].reshape(b, t, ROPE_DIM))
    o_b_p = _dsa_prompt(pp_["qi"], pp_["wi"], pp_["qb"], pp_["ki_b"], pp_["kb_b"], pp_["vb_b"], b, t)
    h_p = _merge(o_lat_p, o_b_p, pp_["gates"], xp2, merge_w)
    y_p = _peer_layer(h_p, peer_w).reshape(b, t, D_MODEL)

    xs2 = x_sample.reshape(db, D_MODEL)
    sp = _project(xs2, jnp.full((1,), past_len, I32), 1, proj_w)
    ql_s = jnp.transpose(sp["ql"][0], (1, 0, 2))
    qr_s = jnp.transpose(sp["qr"][0], (1, 0, 2))
    qi3 = sp["qi"].reshape(db, H_IDX, IDX_DIM)
    o_lat_s, scores = _sample_pass1(
        page_table, ql_s, qr_s, qi3, sp["wi"].reshape(db, H_IDX, 1),
        sp["ckv_b"].reshape(db, 1, KV_LORA), sp["kr_b"].reshape(db, 1, ROPE_DIM),
        cache_kv_latent[l], jnp.swapaxes(cache_k_rope[l], 1, 2), jnp.swapaxes(cache_idx_k[l], 1, 2))
    k_sel_s = min(TOPK_MAX, (past_len + 1) // 4)
    thr, sel_new = _sample_select(scores.reshape(db, past_len), qi3, sp["wi"], sp["ki_b"].reshape(db, 1, IDX_DIM),
                                  k_sel_s)
    o_b_s = _sample_pass2(page_table, thr.reshape(db), sel_new.reshape(db), sp["qb"].reshape(db, H_B, HD_B), scores,
                          sp["kb_b"].reshape(db, 1, kvw), sp["vb_b"].reshape(db, 1, kvw),
                          cache_k[l].reshape(-1, PAGE_SIZE * KV_HEADS_B, HD_B),
                          cache_v[l].reshape(-1, PAGE_SIZE * KV_HEADS_B, HD_B))
    o_lat_s4 = jnp.transpose(o_lat_s, (1, 0, 2)).reshape(1, H_A, db, KV_LORA)
    h_s = _merge(o_lat_s4, o_b_s.reshape(db, H_B * HD_B), sp["gates"], xs2, merge_w)
    y_s = _peer_layer(h_s, peer_w).reshape(db, 1, D_MODEL)

    def rows(d, bb, tt_):
        return (d["ckv"].reshape(1, bb, tt_, KV_LORA), d["kr"].reshape(1, bb, tt_, ROPE_DIM),
                d["kb"].reshape(1, bb, tt_, KV_HEADS_B, HD_B), d["vb"].reshape(1, bb, tt_, KV_HEADS_B, HD_B),
                d["ki"].reshape(1, bb, tt_, IDX_DIM))

    return (y_p, y_s) + rows(pp_, b, t) + rows(sp, db, 1)
```

```python
import functools
import math

import numpy as np
import jax
import jax.numpy as jnp
from jax import lax
from jax.experimental import pallas as pl
from jax.experimental.pallas import tpu as pltpu

F32 = jnp.float32
BF16 = jnp.bfloat16
I32 = jnp.int32

D_MODEL = 2048
PAGE_SIZE = 128
H_A = 8
Q_LORA = 512
KV_LORA = 512
NOPE_DIM = 128
ROPE_DIM = 64
V_DIM = 128
MLA_SCALE = (NOPE_DIM + ROPE_DIM) ** -0.5
H_B = 8
HD_B = 128
KV_HEADS_B = 2
ROT_B = HD_B // 4
DSA_SCALE = HD_B ** -0.5
H_IDX = 16
IDX_DIM = 64
ROT_IDX = IDX_DIM // 4
IDX_SCALE = IDX_DIM ** -0.5
W_IDX_SCALE = H_IDX ** -0.5
TOPK_MAX = 256
PEER_HEADS = 8
N_KEYS = 128
PEER_DQ = 256
PEER_TOPK = 16
ROPE_THETA = 500000.0
RMS_EPS = 1e-6
LN_EPS = 1e-5
DEPTH = 1
ALPHA = (2 * DEPTH) ** 0.25

IN_SIZES = (Q_LORA, KV_LORA + ROPE_DIM, H_B * HD_B, KV_HEADS_B * HD_B, KV_HEADS_B * HD_B,
            H_IDX * IDX_DIM, IDX_DIM, H_IDX, 2 * D_MODEL)
IN_OFFSETS = tuple(int(o) for o in np.cumsum(IN_SIZES)[:-1])

LANE = 128
INT_MIN = -2 ** 31
VMEM_LIMIT = 56 * 1024 * 1024

O_Q = 0
O_C = O_Q + Q_LORA
O_KR = O_C + KV_LORA
O_KB = O_KR + LANE
O_VB = O_KB + KV_HEADS_B * HD_B
O_KI = O_VB + KV_HEADS_B * HD_B
O_WI = O_KI + LANE
P1_COLS = O_WI + LANE

NT_DIMS = (((1,), (1,)), ((), ()))


def _cparams(sem):
    return pltpu.CompilerParams(dimension_semantics=sem, vmem_limit_bytes=VMEM_LIMIT)


def _const_spec(shape):
    n = len(shape)
    return pl.BlockSpec(shape, lambda *_: (0,) * n)


def _rms(x, g):
    return x * lax.rsqrt(jnp.mean(x * x, axis=-1, keepdims=True) + RMS_EPS) * g


def _layer_norm(x, g, b):
    mu = jnp.mean(x, axis=-1, keepdims=True)
    d = x - mu
    var = jnp.mean(d * d, axis=-1, keepdims=True)
    return d * lax.rsqrt(var + LN_EPS) * g + b


def _rope(x, tab_ref, half):
    w = x.shape[-1]
    return (x * tab_ref[0] + pltpu.roll(x, w - half, 1) * tab_ref[1] + pltpu.roll(x, half, 1) * tab_ref[2])


def _rope_table(pos, width, head_dim, rot):
    half = rot // 2
    inv = jnp.power(ROPE_THETA, -jnp.arange(half, dtype=F32) / half)
    ang = pos.astype(F32)[:, None] * inv[None, :]
    cos, sin = jnp.cos(ang), jnp.sin(ang)
    t = pos.shape[0]
    rest = jnp.zeros((t, head_dim - rot), F32)
    zh = jnp.zeros((t, half), F32)
    c = jnp.concatenate([cos, cos, rest + 1.0], axis=1)
    s1 = jnp.concatenate([-sin, zh, rest], axis=1)
    s2 = jnp.concatenate([zh, sin, rest], axis=1)
    return jnp.stack([jnp.tile(a, (1, width // head_dim)) for a in (c, s1, s2)])


def _sort_key(s):
    bits = pltpu.bitcast(s + 0.0, I32)
    return jnp.where(bits < 0, bits ^ 0x7FFFFFFF, bits)


def _kth_largest_key(key, k):
    kf = float(k)

    def count_ge(c):
        return jnp.sum((key >= c).astype(F32), axis=1, keepdims=True)

    ans = jnp.where(count_ge(jnp.zeros_like(key[:, :1])) >= kf, 0, INT_MIN).astype(I32)

    def body(i, ans):
        cand = ans | jnp.left_shift(jnp.int32(1), 30 - i)
        return jnp.where(count_ge(cand) >= kf, cand, ans)

    return lax.fori_loop(0, 31, body, ans)


def _top_rows(vals, k):
    r = vals.shape[0]
    rid = lax.broadcasted_iota(I32, vals.shape, 0).astype(F32)
    tops, ids = [], []
    for _ in range(k):
        m = jnp.max(vals, axis=0, keepdims=True)
        idx = jnp.min(jnp.where(vals == m, rid, float(r)), axis=0, keepdims=True)
        vals = jnp.where(rid == idx, -jnp.inf, vals)
        tops.append(m)
        ids.append(idx)
    return jnp.concatenate(tops, axis=0), jnp.concatenate(ids, axis=0)


def _proj1_kernel(x_ref, w_ref, qg_ref, kvg_ref, ig_ref, ib_ref, tkr_ref, tkb_ref, tki_ref,
                  qn_ref, ckv_ref, kr_ref, kb_ref, vb_ref, ki_ref, wi_ref,
                  ckvb_ref, krb_ref, kbb_ref, vbb_ref, kib_ref):
    z = jnp.dot(x_ref[...].astype(BF16), w_ref[...], preferred_element_type=F32)
    qn_ref[...] = _rms(z[:, O_Q:O_Q + Q_LORA], qg_ref[...]).astype(BF16)
    ckv = _rms(z[:, O_C:O_C + KV_LORA], kvg_ref[...])
    ckv_ref[...] = ckv
    ckvb_ref[...] = ckv.astype(BF16)
    kr = _rope(z[:, O_KR:O_KR + LANE], tkr_ref, ROPE_DIM // 2)[:, :ROPE_DIM]
    kr_ref[...] = kr
    krb_ref[...] = kr.astype(BF16)
    kb = _rope(z[:, O_KB:O_KB + KV_HEADS_B * HD_B], tkb_ref, ROT_B // 2)
    kb_ref[...] = kb
    kbb_ref[...] = kb.astype(BF16)
    vb = z[:, O_VB:O_VB + KV_HEADS_B * HD_B]
    vb_ref[...] = vb
    vbb_ref[...] = vb.astype(BF16)
    zi = z[:, O_KI:O_KI + LANE]
    real = lax.broadcasted_iota(I32, zi.shape, 1) < IDX_DIM
    mu = jnp.sum(zi, axis=-1, keepdims=True) * (1.0 / IDX_DIM)
    d = jnp.where(real, zi - mu, 0.0)
    var = jnp.sum(d * d, axis=-1, keepdims=True) * (1.0 / IDX_DIM)
    ki = _rope(d * lax.rsqrt(var + LN_EPS) * ig_ref[...] + ib_ref[...], tki_ref, ROT_IDX // 2)[:, :IDX_DIM]
    ki_ref[...] = ki
    kib_ref[...] = ki.astype(BF16)
    wi_ref[...] = z[:, O_WI:O_WI + H_IDX] * W_IDX_SCALE


def _proj2_kernel(x_ref, w_ref, tqb_ref, tqi_ref, qb_ref, qi_ref):
    z = jnp.dot(x_ref[...].astype(BF16), w_ref[...], preferred_element_type=F32)
    nb = H_B * HD_B
    qb_ref[...] = _rope(z[:, :nb], tqb_ref, ROT_B // 2).astype(BF16)
    qi_ref[...] = (_rope(z[:, nb:], tqi_ref, ROT_IDX // 2) * IDX_SCALE).astype(BF16)


def _gate_kernel(x_ref, w_ref, g_ref):
    z = jnp.dot(x_ref[...].astype(BF16), w_ref[...], preferred_element_type=F32)
    g_ref[...] = jax.nn.sigmoid(z)


def _qpath_kernel(qn_ref, wuq_ref, wukt_ref, tqr_ref, ql_ref, qr_ref):
    qa = jnp.dot(qn_ref[...], wuq_ref[...], preferred_element_type=F32)
    n_nope = H_A * NOPE_DIM
    qr = _rope(qa[:, n_nope:], tqr_ref, ROPE_DIM // 2).astype(BF16)
    for h in range(H_A):
        qr_ref[0, h] = qr[:, h * ROPE_DIM:(h + 1) * ROPE_DIM]
        nope = qa[:, h * NOPE_DIM:(h + 1) * NOPE_DIM].astype(BF16)
        ql_ref[0, h] = jnp.dot(nope, wukt_ref[h], preferred_element_type=F32).astype(BF16)


def _tab_spec(tab, tm, tiles_per_batch):
    w = tab.shape[-1]
    if tab.shape[1] == 1:
        return pl.BlockSpec((3, 1, w), lambda i: (0, 0, 0))
    return pl.BlockSpec((3, tm, w), lambda i: (0, i % tiles_per_batch, 0))


def _project(x2, pos, n_batch, weights):
    n = x2.shape[0]
    t = pos.shape[0]
    tm = min(256, n if t == 1 else t)
    tpb = max(1, t // tm)
    w1, w2, wg, qg, kvg, ig, ib, wuq, wukt = weights
    tkr = _rope_table(pos, LANE, LANE, ROPE_DIM)
    tkb = _rope_table(pos, KV_HEADS_B * HD_B, HD_B, ROT_B)
    tki = _rope_table(pos, LANE, LANE, ROT_IDX)
    tqb = _rope_table(pos, H_B * HD_B, HD_B, ROT_B)
    tqi = _rope_table(pos, H_IDX * IDX_DIM, IDX_DIM, ROT_IDX)
    tqr = _rope_table(pos, H_A * ROPE_DIM, ROPE_DIM, ROPE_DIM)
    row = lambda w: pl.BlockSpec((tm, w), lambda i: (i, 0))
    xspec = row(D_MODEL)
    kvw = KV_HEADS_B * HD_B
    outs1 = pl.pallas_call(
        _proj1_kernel,
        grid=(n // tm,),
        in_specs=[xspec, _const_spec((D_MODEL, P1_COLS)), _const_spec((1, Q_LORA)), _const_spec((1, KV_LORA)),
                  _const_spec((1, LANE)), _const_spec((1, LANE)),
                  _tab_spec(tkr, tm, tpb), _tab_spec(tkb, tm, tpb), _tab_spec(tki, tm, tpb)],
        out_specs=[row(Q_LORA), row(KV_LORA), row(ROPE_DIM), row(kvw), row(kvw), row(IDX_DIM), row(H_IDX),
                   row(KV_LORA), row(ROPE_DIM), row(kvw), row(kvw), row(IDX_DIM)],
        out_shape=[jax.ShapeDtypeStruct((n, Q_LORA), BF16), jax.ShapeDtypeStruct((n, KV_LORA), F32),
                   jax.ShapeDtypeStruct((n, ROPE_DIM), F32), jax.ShapeDtypeStruct((n, kvw), F32),
                   jax.ShapeDtypeStruct((n, kvw), F32), jax.ShapeDtypeStruct((n, IDX_DIM), F32),
                   jax.ShapeDtypeStruct((n, H_IDX), F32),
                   jax.ShapeDtypeStruct((n, KV_LORA), BF16), jax.ShapeDtypeStruct((n, ROPE_DIM), BF16),
                   jax.ShapeDtypeStruct((n, kvw), BF16), jax.ShapeDtypeStruct((n, kvw), BF16),
                   jax.ShapeDtypeStruct((n, IDX_DIM), BF16)],
        compiler_params=_cparams(("parallel",)),
        name="proj1",
    )(x2, w1, qg, kvg, ig, ib, tkr, tkb, tki)
    qn, ckv, kr, kb, vb, ki, wi, ckv_b, kr_b, kb_b, vb_b, ki_b = outs1

    qb, qi = pl.pallas_call(
        _proj2_kernel,
        grid=(n // tm,),
        in_specs=[xspec, _const_spec(w2.shape), _tab_spec(tqb, tm, tpb), _tab_spec(tqi, tm, tpb)],
        out_specs=[row(H_B * HD_B), row(H_IDX * IDX_DIM)],
        out_shape=[jax.ShapeDtypeStruct((n, H_B * HD_B), BF16), jax.ShapeDtypeStruct((n, H_IDX * IDX_DIM), BF16)],
        compiler_params=_cparams(("parallel",)),
        name="proj2",
    )(x2, w2, tqb, tqi)

    tn = D_MODEL
    tg = min(512, n)
    gates = pl.pallas_call(
        _gate_kernel,
        grid=(2 * D_MODEL // tn, n // tg),
        in_specs=[pl.BlockSpec((tg, D_MODEL), lambda j, i: (i, 0)), pl.BlockSpec((D_MODEL, tn), lambda j, i: (0, j))],
        out_specs=pl.BlockSpec((tg, tn), lambda j, i: (i, j)),
        out_shape=jax.ShapeDtypeStruct((n, 2 * D_MODEL), F32),
        compiler_params=_cparams(("parallel", "parallel")),
        name="gates",
    )(x2, wg)

    t_eff = n // n_batch
    ql, qr = pl.pallas_call(
        _qpath_kernel,
        grid=(n // tm,),
        in_specs=[row(Q_LORA), _const_spec(wuq.shape), _const_spec(wukt.shape), _tab_spec(tqr, tm, tpb)],
        out_specs=[pl.BlockSpec((1, H_A, tm, KV_LORA), lambda i: (i // (t_eff // tm), 0, i % (t_eff // tm), 0)),
                   pl.BlockSpec((1, H_A, tm, ROPE_DIM), lambda i: (i // (t_eff // tm), 0, i % (t_eff // tm), 0))],
        out_shape=[jax.ShapeDtypeStruct((n_batch, H_A, t_eff, KV_LORA), BF16),
                   jax.ShapeDtypeStruct((n_batch, H_A, t_eff, ROPE_DIM), BF16)],
        compiler_params=_cparams(("parallel",)),
        name="qpath",
    )(qn, wuq, wukt, tqr)
    return dict(ckv=ckv, kr=kr, kb=kb, vb=vb, ki=ki, wi=wi, ckv_b=ckv_b, kr_b=kr_b, kb_b=kb_b, vb_b=vb_b,
                ki_b=ki_b, qb=qb, qi=qi, gates=gates, ql=ql, qr=qr)


def _mla_prompt_kernel(ql_ref, qr_ref, c_ref, kr_ref, o_ref, m_sc, l_sc, acc_sc, *, tq, tk):
    i, j = pl.program_id(1), pl.program_id(2)
    rows = H_A * tq

    @pl.when(j == 0)
    def _():
        m_sc[...] = jnp.full_like(m_sc, -jnp.inf)
        l_sc[...] = jnp.zeros_like(l_sc)
        acc_sc[...] = jnp.zeros_like(acc_sc)

    @pl.when(j * tk < (i + 1) * tq)
    def _():
        ql = ql_ref[0].reshape(rows, KV_LORA)
        qr = qr_ref[0].reshape(rows, ROPE_DIM)
        c = c_ref[0]
        s = (lax.dot_general(ql, c, NT_DIMS, preferred_element_type=F32)
             + lax.dot_general(qr, kr_ref[0], NT_DIMS, preferred_element_type=F32)) * MLA_SCALE
        q_pos = i * tq + lax.broadcasted_iota(I32, (H_A, tq, tk), 1).reshape(rows, tk)
        k_pos = j * tk + lax.broadcasted_iota(I32, (rows, tk), 1)
        s = jnp.where(k_pos <= q_pos, s, -jnp.inf)
        m_new = jnp.maximum(m_sc[...], jnp.max(s, axis=1, keepdims=True))
        a = jnp.exp(m_sc[...] - m_new)
        p = jnp.exp(s - m_new)
        l_sc[...] = a * l_sc[...] + jnp.sum(p, axis=1, keepdims=True)
        acc_sc[...] = a * acc_sc[...] + jnp.dot(p.astype(BF16), c, preferred_element_type=F32)
        m_sc[...] = m_new

    @pl.when(j == pl.num_programs(2) - 1)
    def _():
        o_ref[0] = (acc_sc[...] / l_sc[...]).astype(BF16).reshape(H_A, tq, KV_LORA)


def _mla_prompt(ql, qr, ckv_b, kr_b):
    b, _, t, _ = ql.shape
    tq = min(256, t)
    tk = min(512, t)
    last = lambda i, j: jnp.minimum(j, ((i + 1) * tq - 1) // tk)
    return pl.pallas_call(
        functools.partial(_mla_prompt_kernel, tq=tq, tk=tk),
        grid=(b, t // tq, t // tk),
        in_specs=[pl.BlockSpec((1, H_A, tq, KV_LORA), lambda bb, i, j: (bb, 0, i, 0)),
                  pl.BlockSpec((1, H_A, tq, ROPE_DIM), lambda bb, i, j: (bb, 0, i, 0)),
                  pl.BlockSpec((1, tk, KV_LORA), lambda bb, i, j: (bb, last(i, j), 0)),
                  pl.BlockSpec((1, tk, ROPE_DIM), lambda bb, i, j: (bb, last(i, j), 0))],
        out_specs=pl.BlockSpec((1, H_A, tq, KV_LORA), lambda bb, i, j: (bb, 0, i, 0)),
        out_shape=jax.ShapeDtypeStruct((b, H_A, t, KV_LORA), BF16),
        scratch_shapes=[pltpu.VMEM((H_A * tq, 1), F32), pltpu.VMEM((H_A * tq, 1), F32),
                        pltpu.VMEM((H_A * tq, KV_LORA), F32)],
        compiler_params=_cparams(("parallel", "parallel", "arbitrary")),
        name="mla_prompt",
    )(ql, qr, ckv_b, kr_b)


DSA_KEY_SPAN = 512


def _dsa_prompt_kernel(qi_ref, wi_ref, qb_ref, ki_ref, kb_ref, vb_ref, o_ref, *, tq, k_sel):
    i = pl.program_id(1)
    t_full = ki_ref.shape[1]
    span = min(DSA_KEY_SPAN, t_full)
    tiles_per_span = span // tq
    for v in range(t_full // span):
        pl.when(i // tiles_per_span == v)(
            functools.partial(_dsa_prompt_body, qi_ref, wi_ref, qb_ref, ki_ref, kb_ref, vb_ref, o_ref,
                              tq=tq, k_sel=k_sel, t_all=(v + 1) * span))


def _dsa_prompt_body(qi_ref, wi_ref, qb_ref, ki_ref, kb_ref, vb_ref, o_ref, *, tq, k_sel, t_all):
    i = pl.program_id(1)
    ki = ki_ref[0, :t_all, :]
    wi = wi_ref[...]
    score = jnp.zeros((tq, t_all), F32)
    for h in range(H_IDX):
        d = lax.dot_general(qi_ref[:, h * IDX_DIM:(h + 1) * IDX_DIM], ki, NT_DIMS, preferred_element_type=F32)
        score = score + wi[:, h:h + 1] * jnp.maximum(d, 0.0)
    q_pos = i * tq + lax.broadcasted_iota(I32, (tq, t_all), 0)
    k_pos = lax.broadcasted_iota(I32, (tq, t_all), 1)
    causal = k_pos <= q_pos
    key = jnp.where(causal, _sort_key(score), INT_MIN)
    thr = _kth_largest_key(key, k_sel)
    keep = (causal & (key >= thr))[None]
    hpg = H_B // KV_HEADS_B
    for g in range(KV_HEADS_B):
        qg = jnp.concatenate([qb_ref[:, (g * hpg + jj) * HD_B:(g * hpg + jj + 1) * HD_B] for jj in range(hpg)], axis=0)
        kg = kb_ref[0, :t_all, g * HD_B:(g + 1) * HD_B]
        vg = vb_ref[0, :t_all, g * HD_B:(g + 1) * HD_B]
        s = lax.dot_general(qg, kg, NT_DIMS, preferred_element_type=F32) * DSA_SCALE
        s = jnp.where(keep, s.reshape(hpg, tq, t_all), -jnp.inf).reshape(hpg * tq, t_all)
        p = jnp.exp(s - jnp.max(s, axis=1, keepdims=True))
        l = jnp.sum(p, axis=1, keepdims=True)
        o = (jnp.dot(p.astype(BF16), vg, preferred_element_type=F32) / l).astype(BF16)
        for jj in range(hpg):
            o_ref[:, (g * hpg + jj) * HD_B:(g * hpg + jj + 1) * HD_B] = o[jj * tq:(jj + 1) * tq]


def _dsa_prompt(qi, wi, qb, ki_b, kb_b, vb_b, b, t):
    tq = min(128, t)
    k_sel = min(TOPK_MAX, t // 4)
    tpb = t // tq
    row = lambda w: pl.BlockSpec((tq, w), lambda bb, i: (bb * tpb + i, 0))
    per_b = lambda w: pl.BlockSpec((1, t, w), lambda bb, i: (bb, 0, 0))
    kvw = KV_HEADS_B * HD_B
    return pl.pallas_call(
        functools.partial(_dsa_prompt_kernel, tq=tq, k_sel=k_sel),
        grid=(b, tpb),
        in_specs=[row(H_IDX * IDX_DIM), row(H_IDX), row(H_B * HD_B), per_b(IDX_DIM), per_b(kvw), per_b(kvw)],
        out_specs=row(H_B * HD_B),
        out_shape=jax.ShapeDtypeStruct((b * t, H_B * HD_B), BF16),
        compiler_params=_cparams(("parallel", "arbitrary")),
        name="dsa_prompt",
    )(qi, wi, qb, ki_b.reshape(b, t, IDX_DIM), kb_b.reshape(b, t, kvw), vb_b.reshape(b, t, kvw))


def _merge_kernel(ol_ref, ob_ref, g_ref, x_ref, wuv_ref, wao_ref, wbo_ref, wout_ref, lg_ref, lb_ref, h_ref):
    ta = jnp.concatenate([jnp.dot(ol_ref[0, h], wuv_ref[h], preferred_element_type=F32) for h in range(H_A)],
                         axis=1).astype(BF16)
    ya = jnp.dot(ta, wao_ref[...], preferred_element_type=F32)
    yb = jnp.dot(ob_ref[...], wbo_ref[...], preferred_element_type=F32)
    m = (g_ref[:, :D_MODEL] * ya + g_ref[:, D_MODEL:] * yb).astype(BF16)
    mix = jnp.dot(m, wout_ref[...], preferred_element_type=F32)
    h_ref[...] = _layer_norm(ALPHA * x_ref[...] + mix, lg_ref[...], lb_ref[...])


def _merge(o_lat, o_b, gates, x2, weights):
    b, _, t, _ = o_lat.shape
    tm = min(256, t)
    tpb = t // tm
    wuv, wao, wbo, wout, lg, lb = weights
    row = lambda w: pl.BlockSpec((tm, w), lambda i: (i, 0))
    once = lambda a: pl.BlockSpec(a.shape, lambda i: (0,) * a.ndim, pipeline_mode=pl.Buffered(1))
    return pl.pallas_call(
        _merge_kernel,
        grid=(b * tpb,),
        in_specs=[pl.BlockSpec((1, H_A, tm, KV_LORA), lambda i: (i // tpb, 0, i % tpb, 0)),
                  row(H_B * HD_B), row(2 * D_MODEL), row(D_MODEL),
                  once(wuv), once(wao), once(wbo), once(wout), once(lg), once(lb)],
        out_specs=row(D_MODEL),
        out_shape=jax.ShapeDtypeStruct((b * t, D_MODEL), F32),
        compiler_params=_cparams(("parallel",)),
        name="merge",
    )(o_lat, o_b, gates, x2, wuv, wao, wbo, wout, lg, lb)


def _peer_route_kernel(h_ref, wq_ref, sk_ref, e_ref, g_ref):
    q = jnp.dot(h_ref[...].astype(BF16), wq_ref[...], preferred_element_type=F32).astype(BF16)
    half = PEER_DQ // 2
    gates, experts = [], []
    for h in range(PEER_HEADS):
        sv, si = [], []
        for c in range(2):
            qhc = q[:, (2 * h + c) * half:(2 * h + c + 1) * half]
            s_t = lax.dot_general(sk_ref[h, c], qhc, NT_DIMS, preferred_element_type=F32)
            v, ix = _top_rows(s_t, PEER_TOPK)
            sv.append(v)
            si.append(ix)
        kk = PEER_TOPK
        tm = sv[0].shape[1]
        a8 = lax.broadcasted_iota(I32, (8, tm), 0)
        a16 = lax.broadcasted_iota(I32, (kk, tm), 0)
        cand = [sv[0] + sv[1][0:1]]
        cidx = [si[0] * float(N_KEYS) + si[1][0:1]]
        flat = [a16 * kk]
        for b in range(1, 8):
            ok = (a8 + 1) * (b + 1) <= kk
            cand.append(jnp.where(ok, sv[0][0:8] + sv[1][b:b + 1], -jnp.inf))
            cidx.append(si[0][0:8] * float(N_KEYS) + si[1][b:b + 1])
            flat.append(a8 * kk + b)
        cand.append(sv[0][0:1] + sv[1][8:kk])
        cidx.append(si[0][0:1] * float(N_KEYS) + si[1][8:kk])
        flat.append(a8 + 8)
        cand = jnp.concatenate(cand, axis=0)
        cidx = jnp.concatenate(cidx, axis=0)
        rid = jnp.concatenate(flat, axis=0).astype(F32)
        fv, fe = [], []
        for _ in range(PEER_TOPK):
            m = jnp.max(cand, axis=0, keepdims=True)
            pos = jnp.min(jnp.where(cand == m, rid, float(kk * kk)), axis=0, keepdims=True)
            hit = rid == pos
            fe.append(jnp.sum(jnp.where(hit, cidx, 0.0), axis=0, keepdims=True))
            cand = jnp.where(hit, -jnp.inf, cand)
            fv.append(m)
        fv = jnp.concatenate(fv, axis=0)
        p = jnp.exp(fv - fv[0:1])
        gates.append(p / jnp.sum(p, axis=0, keepdims=True))
        experts.append(jnp.concatenate(fe, axis=0))
    g_ref[...] = jnp.concatenate(gates, axis=0).T
    e_ref[...] = jnp.concatenate(experts, axis=0).T.astype(I32)


def _peer_route(h2, wq, sk):
    n = h2.shape[0]
    tm = min(128, n)
    once = lambda a: pl.BlockSpec(a.shape, lambda i: (0,) * a.ndim)
    return pl.pallas_call(
        _peer_route_kernel,
        grid=(n // tm,),
        in_specs=[pl.BlockSpec((tm, D_MODEL), lambda i: (i, 0)), once(wq), once(sk)],
        out_specs=[pl.BlockSpec((tm, N_SEL), lambda i: (i, 0)), pl.BlockSpec((tm, N_SEL), lambda i: (i, 0))],
        out_shape=[jax.ShapeDtypeStruct((n, N_SEL), I32), jax.ShapeDtypeStruct((n, N_SEL), F32)],
        compiler_params=_cparams(("parallel",)),
        name="peer_route",
    )(h2, wq, sk)


N_SEL = PEER_HEADS * PEER_TOPK
PEER_TT = 32
PEER_SLOTS = 4
PEER_AHEAD = PEER_SLOTS - 1
IDX_RING = 3
PEER_CHUNK = 16
HALF_D = D_MODEL // 2


def _pack_bf16_pairs(a):
    b = lax.bitcast_convert_type(a.astype(BF16), jnp.uint16).astype(jnp.uint32)
    h = a.shape[1] // 2
    return lax.bitcast_convert_type(b[:, :h] | (b[:, h:] << 16), I32)


def _low_bf16(w):
    return pltpu.bitcast(w << 16, F32)


def _high_bf16(w):
    return pltpu.bitcast(w & -65536, F32)


def _peer_expert_kernel(e_hbm, h_ref, g_ref, tab_hbm, lg_ref, lb_ref, y_ref,
                        idx_sm, *rest, n_total, n_steps):
    buf, (po_sc, sem_i, sem_g) = rest[:PEER_SLOTS], rest[PEER_SLOTS:]
    i = pl.program_id(0)
    tt = PEER_TT

    def idx_copy(step):
        ring = step % IDX_RING
        return pltpu.make_async_copy(e_hbm.at[step], idx_sm.at[pl.ds(ring * tt, tt)], sem_i.at[ring])

    def row_copy(e, slot, k):
        return pltpu.make_async_copy(tab_hbm.at[e], buf[slot].at[pl.ds(k, 1)], sem_g.at[slot])

    def idx_row(n):
        n = jnp.minimum(n, n_total - 1)
        return ((n // tt) % IDX_RING) * tt + n % tt

    def issue(row, slot, k0, k1):
        for k in range(k0, k1):
            row_copy(idx_sm[row, k], slot, k).start(priority=k % 2)

    def wait(slot):
        other = buf[(slot + 1) % PEER_SLOTS]
        pltpu.make_async_copy(other, buf[slot], sem_g.at[slot]).wait()

    @pl.when(i == 0)
    def _():
        idx_copy(0).start()
        if n_steps > 1:
            idx_copy(1).start()
        idx_copy(0).wait()
        for t in range(PEER_AHEAD):
            issue(idx_row(t), t, 0, N_SEL)

    @pl.when(i + 1 < n_steps)
    def _():
        idx_copy(i + 1).wait()

    @pl.when(i + 2 < n_steps)
    def _():
        idx_copy(i + 2).start()

    n_chunks = N_SEL // PEER_CHUNK
    per_chunk = N_SEL // (2 * n_chunks)

    def token(t, slot):
        wait(slot)
        row = idx_row(i * tt + t + PEER_AHEAD)
        ahead_slot = (slot + PEER_AHEAD) % PEER_SLOTS
        x = h_ref[pl.ds(t, 1), :]
        x_lo, x_hi = x[:, :HALF_D], x[:, HALF_D:]
        words = buf[slot]
        hk = []
        for c in range(n_chunks):
            wu = words[c * PEER_CHUNK:(c + 1) * PEER_CHUNK, :HALF_D]
            hk.append(jnp.sum(_low_bf16(wu) * x_lo + _high_bf16(wu) * x_hi, axis=1, keepdims=True))
            issue(row, ahead_slot, c * per_chunk, (c + 1) * per_chunk)
        hk = jnp.concatenate(hk, axis=0)
        gcol = jnp.transpose(jnp.broadcast_to(g_ref[pl.ds(t, 1), :], (N_SEL, N_SEL)))[:, 0:1]
        a = gcol * (0.5 * hk * (1.0 + lax.erf(hk * math.sqrt(0.5))))
        acc_lo = jnp.zeros((8, HALF_D), F32)
        acc_hi = jnp.zeros((8, HALF_D), F32)
        for c in range(n_chunks):
            wv = words[c * PEER_CHUNK:(c + 1) * PEER_CHUNK, HALF_D:]
            ac = a[c * PEER_CHUNK:(c + 1) * PEER_CHUNK]
            for r in range(0, PEER_CHUNK, 8):
                acc_lo = acc_lo + _low_bf16(wv[r:r + 8]) * ac[r:r + 8]
                acc_hi = acc_hi + _high_bf16(wv[r:r + 8]) * ac[r:r + 8]
            issue(row, ahead_slot, (n_chunks + c) * per_chunk, (n_chunks + c + 1) * per_chunk)
        po_sc[pl.ds(t, 1), :] = jnp.concatenate(
            [jnp.sum(acc_lo, axis=0, keepdims=True), jnp.sum(acc_hi, axis=0, keepdims=True)], axis=1)

    def group(g, carry):
        for j in range(PEER_SLOTS):
            token(g * PEER_SLOTS + j, j)
        return carry

    lax.fori_loop(0, tt // PEER_SLOTS, group, 0)

    @pl.when(i == n_steps - 1)
    def _():
        for j in range(PEER_AHEAD):
            wait(j)

    y_ref[...] = _layer_norm(ALPHA * h_ref[...] + po_sc[...], lg_ref[...], lb_ref[...])


def _peer_experts(eidx, gate, h2, table, lg, lb):
    n = h2.shape[0]
    tt = PEER_TT
    assert n % tt == 0 and tt % PEER_SLOTS == 0
    steps = n // tt
    row = lambda w: pl.BlockSpec((tt, w), lambda i: (i, 0))
    anyspec = pl.BlockSpec(memory_space=pl.ANY)
    return pl.pallas_call(
        functools.partial(_peer_expert_kernel, n_total=n, n_steps=steps),
        grid=(steps,),
        in_specs=[anyspec, row(D_MODEL), row(N_SEL), anyspec,
                  _const_spec((1, D_MODEL)), _const_spec((1, D_MODEL))],
        out_specs=row(D_MODEL),
        out_shape=jax.ShapeDtypeStruct((n, D_MODEL), F32),
        scratch_shapes=[pltpu.SMEM((IDX_RING * tt, N_SEL), I32),
                        *[pltpu.VMEM((N_SEL, D_MODEL), I32) for _ in range(PEER_SLOTS)],
                        pltpu.VMEM((tt, D_MODEL), F32),
                        pltpu.SemaphoreType.DMA((IDX_RING,)),
                        pltpu.SemaphoreType.DMA((PEER_SLOTS,))],
        compiler_params=_cparams(("arbitrary",)),
        name="peer_experts",
    )(eidx.reshape(steps, tt, N_SEL), h2, gate, table, lg, lb)


def _peer_layer(h2, peer_w):
    wq, sk, table, lg, lb = peer_w
    eidx, gate = _peer_route(h2, wq, sk)
    return _peer_experts(eidx, gate, h2, table, lg, lb)


PAGES_PER_STEP = 16


def _sample_pass1_kernel(pt_ref, ql_ref, qr_ref, qi_ref, wi_ref, cn_ref, krn_ref, *rest):
    pp = PAGES_PER_STEP
    c_refs, kr_refs, ki_refs = rest[:pp], rest[pp:2 * pp], rest[2 * pp:3 * pp]
    o_ref, sc_ref, m_sc, l_sc, acc_sc = rest[3 * pp:]
    p_id = pl.program_id(1)

    @pl.when(p_id == 0)
    def _():
        m_sc[...] = jnp.full_like(m_sc, -jnp.inf)
        l_sc[...] = jnp.zeros_like(l_sc)
        acc_sc[...] = jnp.zeros_like(acc_sc)

    ql, qr = ql_ref[0], qr_ref[0]

    def absorb(s, weighted_values):
        s = s * MLA_SCALE
        m_new = jnp.maximum(m_sc[...], jnp.max(s, axis=1, keepdims=True))
        a = jnp.exp(m_sc[...] - m_new)
        p = jnp.exp(s - m_new)
        l_sc[...] = a * l_sc[...] + jnp.sum(p, axis=1, keepdims=True)
        acc_sc[...] = a * acc_sc[...] + weighted_values(p.astype(BF16))
        m_sc[...] = m_new

    c = jnp.concatenate([r[0].astype(BF16) for r in c_refs], axis=0)
    kr_t = jnp.concatenate([r[0].astype(BF16) for r in kr_refs], axis=1)
    absorb(lax.dot_general(ql, c, NT_DIMS, preferred_element_type=F32)
           + jnp.dot(qr, kr_t, preferred_element_type=F32),
           lambda p: jnp.dot(p, c, preferred_element_type=F32))
    ki_t = jnp.concatenate([r[0].astype(BF16) for r in ki_refs], axis=1)
    d = jnp.dot(qi_ref[0], ki_t, preferred_element_type=F32)
    sc_ref[0] = jnp.sum(wi_ref[0] * jnp.maximum(d, 0.0), axis=0, keepdims=True)

    @pl.when(p_id == pl.num_programs(1) - 1)
    def _():
        cn = cn_ref[0].astype(F32)
        s_new = (jnp.sum(ql.astype(F32) * cn, axis=1, keepdims=True)
                 + jnp.sum(qr.astype(F32) * krn_ref[0].astype(F32), axis=1, keepdims=True))
        absorb(s_new, lambda p: p.astype(F32) * cn)
        o_ref[0] = (acc_sc[...] / l_sc[...]).astype(BF16)


def _sample_pass1(page_table, ql, qr, qi3, wi3, ckv_new, kr_new, c_pool, kr_pool, ki_pool):
    db, n_pages = page_table.shape
    pp = PAGES_PER_STEP
    per_b = lambda a: pl.BlockSpec((1,) + a.shape[1:], lambda b, p, pt: (b,) + (0,) * (a.ndim - 1))

    def page_specs(*page_shape):
        return [pl.BlockSpec((1,) + page_shape, lambda b, p, pt, j=j: (pt[b, p * pp + j], 0, 0))
                for j in range(pp)]

    grid_spec = pltpu.PrefetchScalarGridSpec(
        num_scalar_prefetch=1,
        grid=(db, n_pages // pp),
        in_specs=[per_b(ql), per_b(qr), per_b(qi3), per_b(wi3), per_b(ckv_new), per_b(kr_new)]
                 + page_specs(PAGE_SIZE, KV_LORA) + page_specs(ROPE_DIM, PAGE_SIZE) + page_specs(IDX_DIM, PAGE_SIZE),
        out_specs=[pl.BlockSpec((1, H_A, KV_LORA), lambda b, p, pt: (b, 0, 0)),
                   pl.BlockSpec((1, 1, pp * PAGE_SIZE), lambda b, p, pt: (b, 0, p))],
        scratch_shapes=[pltpu.VMEM((H_A, 1), F32), pltpu.VMEM((H_A, 1), F32), pltpu.VMEM((H_A, KV_LORA), F32)],
    )
    return pl.pallas_call(
        _sample_pass1_kernel,
        grid_spec=grid_spec,
        out_shape=[jax.ShapeDtypeStruct((db, H_A, KV_LORA), BF16),
                   jax.ShapeDtypeStruct((db, 1, n_pages * PAGE_SIZE), F32)],
        compiler_params=_cparams(("parallel", "arbitrary")),
        name="sample_pass1",
    )(page_table, ql, qr, qi3, wi3, ckv_new, kr_new, *([c_pool] * pp), *([kr_pool] * pp), *([ki_pool] * pp))


def _sample_select_kernel(sc_ref, qi_ref, wi_ref, kin_ref, thr_ref, new_ref, *, k_sel):
    past = sc_ref[...]
    qi = qi_ref[...].astype(F32)
    kn = kin_ref[...].astype(F32)
    d = jnp.sum(qi * kn, axis=2)
    s_new = jnp.sum(wi_ref[...] * jnp.maximum(d, 0.0), axis=1, keepdims=True)
    key_new = _sort_key(s_new)
    lane0 = lax.broadcasted_iota(I32, (past.shape[0], LANE), 1) == 0
    key = jnp.concatenate([_sort_key(past), jnp.where(lane0, key_new, INT_MIN)], axis=1)
    thr = _kth_largest_key(key, k_sel)
    thr_ref[...] = thr
    new_ref[...] = (key_new >= thr).astype(I32)


def _sample_select(scores, qi3, wi2, ki_new3, k_sel):
    db, past = scores.shape
    rows = min(8, db)
    return pl.pallas_call(
        functools.partial(_sample_select_kernel, k_sel=k_sel),
        grid=(db // rows,),
        in_specs=[pl.BlockSpec((rows, past), lambda i: (i, 0)),
                  pl.BlockSpec((rows, H_IDX, IDX_DIM), lambda i: (i, 0, 0)),
                  pl.BlockSpec((rows, H_IDX), lambda i: (i, 0)),
                  pl.BlockSpec((rows, 1, IDX_DIM), lambda i: (i, 0, 0))],
        out_specs=[pl.BlockSpec((rows, 1), lambda i: (i, 0)), pl.BlockSpec((rows, 1), lambda i: (i, 0))],
        out_shape=[jax.ShapeDtypeStruct((db, 1), I32), jax.ShapeDtypeStruct((db, 1), I32)],
        compiler_params=_cparams(("parallel",)),
        name="sample_select",
    )(scores, qi3, wi2, ki_new3)


def _sample_pass2_kernel(pt_ref, thr_ref, new_ref, qb_ref, sc_ref, kn_ref, vn_ref, *rest):
    pp = PAGES_PER_STEP
    k_refs, v_refs = rest[:pp], rest[pp:2 * pp]
    o_ref, m_sc, l_sc, acc_sc = rest[2 * pp:]
    b, p_id = pl.program_id(0), pl.program_id(1)
    hpg = H_B // KV_HEADS_B

    @pl.when(p_id == 0)
    def _():
        m_sc[...] = jnp.full_like(m_sc, -jnp.inf)
        l_sc[...] = jnp.zeros_like(l_sc)
        acc_sc[...] = jnp.zeros_like(acc_sc)

    def absorb(g, s, keep, weighted_values):
        rows = slice(g * hpg, (g + 1) * hpg)
        s = jnp.where(keep, s * DSA_SCALE, -jnp.inf)
        m_old = m_sc[rows]
        m_new = jnp.maximum(m_old, jnp.max(s, axis=1, keepdims=True))
        m_safe = jnp.where(m_new == -jnp.inf, 0.0, m_new)
        a = jnp.exp(m_old - m_safe)
        p = jnp.exp(s - m_safe)
        l_sc[rows] = a * l_sc[rows] + jnp.sum(p, axis=1, keepdims=True)
        acc_sc[rows] = a * acc_sc[rows] + weighted_values(p.astype(BF16))
        m_sc[rows] = m_new

    keep = _sort_key(sc_ref[0]) >= thr_ref[b]
    for g in range(KV_HEADS_B):
        own = pl.ds(g, PAGE_SIZE, stride=KV_HEADS_B)
        k = jnp.concatenate([r[0, own, :].astype(BF16) for r in k_refs], axis=0)
        v = jnp.concatenate([r[0, own, :].astype(BF16) for r in v_refs], axis=0)
        absorb(g, lax.dot_general(qb_ref[0, g * hpg:(g + 1) * hpg], k, NT_DIMS, preferred_element_type=F32),
               keep, lambda p, v=v: jnp.dot(p, v, preferred_element_type=F32))

    @pl.when(p_id == pl.num_programs(1) - 1)
    def _():
        for g in range(KV_HEADS_B):
            cols = slice(g * HD_B, (g + 1) * HD_B)
            kn = kn_ref[0, :, cols].astype(F32)
            vn = vn_ref[0, :, cols].astype(F32)
            s_new = jnp.sum(qb_ref[0, g * hpg:(g + 1) * hpg].astype(F32) * kn, axis=1, keepdims=True)
            absorb(g, s_new, new_ref[b] > 0, lambda p, vn=vn: p.astype(F32) * vn)
        o_ref[0] = (acc_sc[...] / l_sc[...]).astype(BF16)


def _sample_pass2(page_table, thr, sel_new, qb3, scores3, kb_new, vb_new, k_pool, v_pool):
    db, n_pages = page_table.shape
    pp = PAGES_PER_STEP
    kvw = KV_HEADS_B * HD_B
    per_b = lambda a: pl.BlockSpec((1,) + a.shape[1:], lambda b, p, *_: (b,) + (0,) * (a.ndim - 1))
    pages = [pl.BlockSpec((1, PAGE_SIZE * KV_HEADS_B, HD_B), lambda b, p, pt, th, nw, j=j: (pt[b, p * pp + j], 0, 0))
             for j in range(pp)]
    grid_spec = pltpu.PrefetchScalarGridSpec(
        num_scalar_prefetch=3,
        grid=(db, n_pages // pp),
        in_specs=[per_b(qb3), pl.BlockSpec((1, 1, pp * PAGE_SIZE), lambda b, p, *_: (b, 0, p)),
                  per_b(kb_new), per_b(vb_new)] + pages + pages,
        out_specs=pl.BlockSpec((1, H_B, HD_B), lambda b, p, *_: (b, 0, 0)),
        scratch_shapes=[pltpu.VMEM((H_B, 1), F32), pltpu.VMEM((H_B, 1), F32), pltpu.VMEM((H_B, HD_B), F32)],
    )
    return pl.pallas_call(
        _sample_pass2_kernel,
        grid_spec=grid_spec,
        out_shape=jax.ShapeDtypeStruct((db, H_B, HD_B), BF16),
        compiler_params=_cparams(("parallel", "arbitrary")),
        name="sample_pass2",
    )(page_table, thr, sel_new, qb3, scores3, kb_new, vb_new, *([k_pool] * pp), *([v_pool] * pp))


def kernel(x_prompt, x_sample, cache_kv_latent, cache_k_rope, cache_k, cache_v, cache_idx_k, page_table, w_in, mla_q_norm, mla_w_uq, mla_kv_norm, mla_w_uk, mla_w_uv, mla_w_o, idx_k_norm_g, idx_k_norm_b, dsa_w_o, w_out, ln1_g, ln1_b, peer_w_q, peer_sub_keys, peer_u, peer_v, ln2_g, ln2_b):
    b, t, _ = x_prompt.shape
    db, ts, _ = x_sample.shape
    assert ts == 1, "the sample group decodes one token per sequence"
    n_pages = page_table.shape[1]
    past_len = n_pages * PAGE_SIZE
    l = 0

    wq_, wkv_, wbq_, wbk_, wbv_, wiq_, wik_, wiw_, wg_ = jnp.split(w_in[l], IN_OFFSETS, axis=1)
    padc = lambda a: jnp.pad(a, ((0, 0), (0, LANE - a.shape[1])))
    w1 = jnp.concatenate([wq_, wkv_[:, :KV_LORA], padc(wkv_[:, KV_LORA:]), wbk_, wbv_, padc(wik_), padc(wiw_)],
                         axis=1).astype(BF16)
    w2 = jnp.concatenate([wbq_, wiq_], axis=1).astype(BF16)
    wg = wg_.astype(BF16)
    wuq = jnp.concatenate([mla_w_uq[l][:, :, :NOPE_DIM].reshape(Q_LORA, H_A * NOPE_DIM),
                           mla_w_uq[l][:, :, NOPE_DIM:].reshape(Q_LORA, H_A * ROPE_DIM)], axis=1).astype(BF16)
    wukt = jnp.transpose(mla_w_uk[l], (1, 2, 0)).astype(BF16)
    row1 = lambda a: a.reshape(1, -1).astype(F32)
    pad1 = lambda a: jnp.pad(a.astype(F32), (0, LANE - a.shape[0])).reshape(1, LANE)
    proj_w = (w1, w2, wg, row1(mla_q_norm[l]), row1(mla_kv_norm[l]), pad1(idx_k_norm_g[l]), pad1(idx_k_norm_b[l]),
              wuq, wukt)
    merge_w = (jnp.transpose(mla_w_uv[l], (1, 0, 2)).astype(BF16), mla_w_o[l].astype(BF16),
               dsa_w_o[l].astype(BF16), w_out[l].astype(BF16), row1(ln1_g[l]), row1(ln1_b[l]))
    peer_table = jnp.concatenate([_pack_bf16_pairs(peer_u[l]), _pack_bf16_pairs(peer_v[l])], axis=1)
    peer_table = peer_table.reshape(peer_table.shape[0], 1, D_MODEL)
    peer_w = (peer_w_q[l].astype(BF16), peer_sub_keys[l].astype(BF16), peer_table, row1(ln2_g[l]), row1(ln2_b[l]))
    kvw = KV_HEADS_B * HD_B

    xp2 = x_prompt.reshape(b * t, D_MODEL)
    pp_ = _project(xp2, jnp.arange(t, dtype=I32), b, proj_w)
    o_lat_p = _mla_prompt(pp_["ql"], pp_["qr"], pp_["ckv_b"].reshape(b, t, KV_LORA), pp_["kr_b"].reshape(b, t, ROPE_DIM))
    o_b_p = _dsa_prompt(pp_["qi"], pp_["wi"], pp_["qb"], pp_["ki_b"], pp_["kb_b"], pp_["vb_b"], b, t)
    h_p = _merge(o_lat_p, o_b_p, pp_["gates"], xp2, merge_w)
    y_p = _peer_layer(h_p, peer_w).reshape(b, t, D_MODEL)

    xs2 = x_sample.reshape(db, D_MODEL)
    sp = _project(xs2, jnp.full((1,), past_len, I32), 1, proj_w)
    ql_s = jnp.transpose(sp["ql"][0], (1, 0, 2))
    qr_s = jnp.transpose(sp["qr"][0], (1, 0, 2))
    qi3 = sp["qi"].reshape(db, H_IDX, IDX_DIM)
    o_lat_s, scores = _sample_pass1(
        page_table, ql_s, qr_s, qi3, sp["wi"].reshape(db, H_IDX, 1),
        sp["ckv_b"].reshape(db, 1, KV_LORA), sp["kr_b"].reshape(db, 1, ROPE_DIM),
        cache_kv_latent[l], jnp.swapaxes(cache_k_rope[l], 1, 2), jnp.swapaxes(cache_idx_k[l], 1, 2))
    k_sel_s = min(TOPK_MAX, (past_len + 1) // 4)
    thr, sel_new = _sample_select(scores.reshape(db, past_len), qi3, sp["wi"], sp["ki_b"].reshape(db, 1, IDX_DIM),
                                  k_sel_s)
    o_b_s = _sample_pass2(page_table, thr.reshape(db), sel_new.reshape(db), sp["qb"].reshape(db, H_B, HD_B), scores,
                          sp["kb_b"].reshape(db, 1, kvw), sp["vb_b"].reshape(db, 1, kvw),
                          cache_k[l].reshape(-1, PAGE_SIZE * KV_HEADS_B, HD_B),
                          cache_v[l].reshape(-1, PAGE_SIZE * KV_HEADS_B, HD_B))
    o_lat_s4 = jnp.transpose(o_lat_s, (1, 0, 2)).reshape(1, H_A, db, KV_LORA)
    h_s = _merge(o_lat_s4, o_b_s.reshape(db, H_B * HD_B), sp["gates"], xs2, merge_w)
    y_s = _peer_layer(h_s, peer_w).reshape(db, 1, D_MODEL)

    def rows(d, bb, tt_):
        return (d["ckv"].reshape(1, bb, tt_, KV_LORA), d["kr"].reshape(1, bb, tt_, ROPE_DIM),
                d["kb"].reshape(1, bb, tt_, KV_HEADS_B, HD_B), d["vb"].reshape(1, bb, tt_, KV_HEADS_B, HD_B),
                d["ki"].reshape(1, bb, tt_, IDX_DIM))

    return (y_p, y_s) + rows(pp_, b, t) + rows(sp, db, 1)
```

```python
import functools
import math

import numpy as np
import jax
import jax.numpy as jnp
from jax import lax
from jax.experimental import pallas as pl
from jax.experimental.pallas import tpu as pltpu

F32 = jnp.float32
BF16 = jnp.bfloat16
I32 = jnp.int32

D_MODEL = 2048
PAGE_SIZE = 128
H_A = 8
Q_LORA = 512
KV_LORA = 512
NOPE_DIM = 128
ROPE_DIM = 64
V_DIM = 128
MLA_SCALE = (NOPE_DIM + ROPE_DIM) ** -0.5
H_B = 8
HD_B = 128
KV_HEADS_B = 2
ROT_B = HD_B // 4
DSA_SCALE = HD_B ** -0.5
H_IDX = 16
IDX_DIM = 64
ROT_IDX = IDX_DIM // 4
IDX_SCALE = IDX_DIM ** -0.5
W_IDX_SCALE = H_IDX ** -0.5
TOPK_MAX = 256
PEER_HEADS = 8
N_KEYS = 128
PEER_DQ = 256
PEER_TOPK = 16
ROPE_THETA = 500000.0
RMS_EPS = 1e-6
LN_EPS = 1e-5
DEPTH = 1
ALPHA = (2 * DEPTH) ** 0.25

IN_SIZES = (Q_LORA, KV_LORA + ROPE_DIM, H_B * HD_B, KV_HEADS_B * HD_B, KV_HEADS_B * HD_B,
            H_IDX * IDX_DIM, IDX_DIM, H_IDX, 2 * D_MODEL)
IN_OFFSETS = tuple(int(o) for o in np.cumsum(IN_SIZES)[:-1])

LANE = 128
INT_MIN = -2 ** 31
VMEM_LIMIT = 56 * 1024 * 1024

O_Q = 0
O_C = O_Q + Q_LORA
O_KR = O_C + KV_LORA
O_KB = O_KR + LANE
O_VB = O_KB + KV_HEADS_B * HD_B
O_KI = O_VB + KV_HEADS_B * HD_B
O_WI = O_KI + LANE
P1_COLS = O_WI + LANE

NT_DIMS = (((1,), (1,)), ((), ()))


def _cparams(sem):
    return pltpu.CompilerParams(dimension_semantics=sem, vmem_limit_bytes=VMEM_LIMIT)


def _const_spec(shape):
    n = len(shape)
    return pl.BlockSpec(shape, lambda *_: (0,) * n)


def _rms(x, g):
    return x * lax.rsqrt(jnp.mean(x * x, axis=-1, keepdims=True) + RMS_EPS) * g


def _layer_norm(x, g, b):
    mu = jnp.mean(x, axis=-1, keepdims=True)
    d = x - mu
    var = jnp.mean(d * d, axis=-1, keepdims=True)
    return d * lax.rsqrt(var + LN_EPS) * g + b


def _rope(x, tab_ref, half):
    w = x.shape[-1]
    return (x * tab_ref[0] + pltpu.roll(x, w - half, 1) * tab_ref[1] + pltpu.roll(x, half, 1) * tab_ref[2])


def _rope_table(pos, width, head_dim, rot):
    half = rot // 2
    inv = jnp.power(ROPE_THETA, -jnp.arange(half, dtype=F32) / half)
    ang = pos.astype(F32)[:, None] * inv[None, :]
    cos, sin = jnp.cos(ang), jnp.sin(ang)
    t = pos.shape[0]
    rest = jnp.zeros((t, head_dim - rot), F32)
    zh = jnp.zeros((t, half), F32)
    c = jnp.concatenate([cos, cos, rest + 1.0], axis=1)
    s1 = jnp.concatenate([-sin, zh, rest], axis=1)
    s2 = jnp.concatenate([zh, sin, rest], axis=1)
    return jnp.stack([jnp.tile(a, (1, width // head_dim)) for a in (c, s1, s2)])


def _sort_key(s):
    bits = pltpu.bitcast(s + 0.0, I32)
    return jnp.where(bits < 0, bits ^ 0x7FFFFFFF, bits)


def _kth_largest_key(key, k):
    kf = float(k)

    def count_ge(c):
        return jnp.sum((key >= c).astype(F32), axis=1, keepdims=True)

    ans = jnp.where(count_ge(jnp.zeros_like(key[:, :1])) >= kf, 0, INT_MIN).astype(I32)

    def body(i, ans):
        cand = ans | jnp.left_shift(jnp.int32(1), 30 - i)
        return jnp.where(count_ge(cand) >= kf, cand, ans)

    return lax.fori_loop(0, 31, body, ans)


def _top_rows(vals, k):
    r = vals.shape[0]
    rid = lax.broadcasted_iota(I32, vals.shape, 0).astype(F32)
    tops, ids = [], []
    for _ in range(k):
        m = jnp.max(vals, axis=0, keepdims=True)
        idx = jnp.min(jnp.where(vals == m, rid, float(r)), axis=0, keepdims=True)
        vals = jnp.where(rid == idx, -jnp.inf, vals)
        tops.append(m)
        ids.append(idx)
    return jnp.concatenate(tops, axis=0), jnp.concatenate(ids, axis=0)


def _proj1_kernel(x_ref, w_ref, qg_ref, kvg_ref, ig_ref, ib_ref, tkr_ref, tkb_ref, tki_ref,
                  qn_ref, ckv_ref, kr_ref, kb_ref, vb_ref, ki_ref, wi_ref,
                  ckvb_ref, krb_ref, kbb_ref, vbb_ref, kib_ref):
    z = jnp.dot(x_ref[...].astype(BF16), w_ref[...], preferred_element_type=F32)
    qn_ref[...] = _rms(z[:, O_Q:O_Q + Q_LORA], qg_ref[...]).astype(BF16)
    ckv = _rms(z[:, O_C:O_C + KV_LORA], kvg_ref[...])
    ckv_ref[...] = ckv
    ckvb_ref[...] = ckv.astype(BF16)
    kr = _rope(z[:, O_KR:O_KR + LANE], tkr_ref, ROPE_DIM // 2)[:, :ROPE_DIM]
    kr_ref[...] = kr
    krb_ref[...] = kr.astype(BF16)
    kb = _rope(z[:, O_KB:O_KB + KV_HEADS_B * HD_B], tkb_ref, ROT_B // 2)
    kb_ref[...] = kb
    kbb_ref[...] = kb.astype(BF16)
    vb = z[:, O_VB:O_VB + KV_HEADS_B * HD_B]
    vb_ref[...] = vb
    vbb_ref[...] = vb.astype(BF16)
    zi = z[:, O_KI:O_KI + LANE]
    real = lax.broadcasted_iota(I32, zi.shape, 1) < IDX_DIM
    mu = jnp.sum(zi, axis=-1, keepdims=True) * (1.0 / IDX_DIM)
    d = jnp.where(real, zi - mu, 0.0)
    var = jnp.sum(d * d, axis=-1, keepdims=True) * (1.0 / IDX_DIM)
    ki = _rope(d * lax.rsqrt(var + LN_EPS) * ig_ref[...] + ib_ref[...], tki_ref, ROT_IDX // 2)[:, :IDX_DIM]
    ki_ref[...] = ki
    kib_ref[...] = ki.astype(BF16)
    wi_ref[...] = z[:, O_WI:O_WI + H_IDX] * W_IDX_SCALE


def _proj2_kernel(x_ref, w_ref, tqb_ref, tqi_ref, qb_ref, qi_ref):
    z = jnp.dot(x_ref[...].astype(BF16), w_ref[...], preferred_element_type=F32)
    nb = H_B * HD_B
    qb_ref[...] = _rope(z[:, :nb], tqb_ref, ROT_B // 2).astype(BF16)
    qi_ref[...] = (_rope(z[:, nb:], tqi_ref, ROT_IDX // 2) * IDX_SCALE).astype(BF16)


def _gate_kernel(x_ref, w_ref, g_ref):
    z = jnp.dot(x_ref[...].astype(BF16), w_ref[...], preferred_element_type=F32)
    g_ref[...] = jax.nn.sigmoid(z)


def _qpath_kernel(qn_ref, wuq_ref, wukt_ref, tqr_ref, ql_ref, qr_ref):
    qa = jnp.dot(qn_ref[...], wuq_ref[...], preferred_element_type=F32)
    n_nope = H_A * NOPE_DIM
    qr = _rope(qa[:, n_nope:], tqr_ref, ROPE_DIM // 2).astype(BF16)
    for h in range(H_A):
        qr_ref[0, h] = qr[:, h * ROPE_DIM:(h + 1) * ROPE_DIM]
        nope = qa[:, h * NOPE_DIM:(h + 1) * NOPE_DIM].astype(BF16)
        ql_ref[0, h] = jnp.dot(nope, wukt_ref[h], preferred_element_type=F32).astype(BF16)


def _tab_spec(tab, tm, tiles_per_batch):
    w = tab.shape[-1]
    if tab.shape[1] == 1:
        return pl.BlockSpec((3, 1, w), lambda i: (0, 0, 0))
    return pl.BlockSpec((3, tm, w), lambda i: (0, i % tiles_per_batch, 0))


def _project(x2, pos, n_batch, weights):
    n = x2.shape[0]
    t = pos.shape[0]
    tm = min(256, n if t == 1 else t)
    tpb = max(1, t // tm)
    w1, w2, wg, qg, kvg, ig, ib, wuq, wukt = weights
    tkr = _rope_table(pos, LANE, LANE, ROPE_DIM)
    tkb = _rope_table(pos, KV_HEADS_B * HD_B, HD_B, ROT_B)
    tki = _rope_table(pos, LANE, LANE, ROT_IDX)
    tqb = _rope_table(pos, H_B * HD_B, HD_B, ROT_B)
    tqi = _rope_table(pos, H_IDX * IDX_DIM, IDX_DIM, ROT_IDX)
    tqr = _rope_table(pos, H_A * ROPE_DIM, ROPE_DIM, ROPE_DIM)
    row = lambda w: pl.BlockSpec((tm, w), lambda i: (i, 0))
    xspec = row(D_MODEL)
    kvw = KV_HEADS_B * HD_B
    outs1 = pl.pallas_call(
        _proj1_kernel,
        grid=(n // tm,),
        in_specs=[xspec, _const_spec((D_MODEL, P1_COLS)), _const_spec((1, Q_LORA)), _const_spec((1, KV_LORA)),
                  _const_spec((1, LANE)), _const_spec((1, LANE)),
                  _tab_spec(tkr, tm, tpb), _tab_spec(tkb, tm, tpb), _tab_spec(tki, tm, tpb)],
        out_specs=[row(Q_LORA), row(KV_LORA), row(ROPE_DIM), row(kvw), row(kvw), row(IDX_DIM), row(H_IDX),
                   row(KV_LORA), row(ROPE_DIM), row(kvw), row(kvw), row(IDX_DIM)],
        out_shape=[jax.ShapeDtypeStruct((n, Q_LORA), BF16), jax.ShapeDtypeStruct((n, KV_LORA), F32),
                   jax.ShapeDtypeStruct((n, ROPE_DIM), F32), jax.ShapeDtypeStruct((n, kvw), F32),
                   jax.ShapeDtypeStruct((n, kvw), F32), jax.ShapeDtypeStruct((n, IDX_DIM), F32),
                   jax.ShapeDtypeStruct((n, H_IDX), F32),
                   jax.ShapeDtypeStruct((n, KV_LORA), BF16), jax.ShapeDtypeStruct((n, ROPE_DIM), BF16),
                   jax.ShapeDtypeStruct((n, kvw), BF16), jax.ShapeDtypeStruct((n, kvw), BF16),
                   jax.ShapeDtypeStruct((n, IDX_DIM), BF16)],
        compiler_params=_cparams(("parallel",)),
        name="proj1",
    )(x2, w1, qg, kvg, ig, ib, tkr, tkb, tki)
    qn, ckv, kr, kb, vb, ki, wi, ckv_b, kr_b, kb_b, vb_b, ki_b = outs1

    qb, qi = pl.pallas_call(
        _proj2_kernel,
        grid=(n // tm,),
        in_specs=[xspec, _const_spec(w2.shape), _tab_spec(tqb, tm, tpb), _tab_spec(tqi, tm, tpb)],
        out_specs=[row(H_B * HD_B), row(H_IDX * IDX_DIM)],
        out_shape=[jax.ShapeDtypeStruct((n, H_B * HD_B), BF16), jax.ShapeDtypeStruct((n, H_IDX * IDX_DIM), BF16)],
        compiler_params=_cparams(("parallel",)),
        name="proj2",
    )(x2, w2, tqb, tqi)

    tn = D_MODEL
    tg = min(512, n)
    gates = pl.pallas_call(
        _gate_kernel,
        grid=(2 * D_MODEL // tn, n // tg),
        in_specs=[pl.BlockSpec((tg, D_MODEL), lambda j, i: (i, 0)), pl.BlockSpec((D_MODEL, tn), lambda j, i: (0, j))],
        out_specs=pl.BlockSpec((tg, tn), lambda j, i: (i, j)),
        out_shape=jax.ShapeDtypeStruct((n, 2 * D_MODEL), F32),
        compiler_params=_cparams(("parallel", "parallel")),
        name="gates",
    )(x2, wg)

    t_eff = n // n_batch
    ql, qr = pl.pallas_call(
        _qpath_kernel,
        grid=(n // tm,),
        in_specs=[row(Q_LORA), _const_spec(wuq.shape), _const_spec(wukt.shape), _tab_spec(tqr, tm, tpb)],
        out_specs=[pl.BlockSpec((1, H_A, tm, KV_LORA), lambda i: (i // (t_eff // tm), 0, i % (t_eff // tm), 0)),
                   pl.BlockSpec((1, H_A, tm, ROPE_DIM), lambda i: (i // (t_eff // tm), 0, i % (t_eff // tm), 0))],
        out_shape=[jax.ShapeDtypeStruct((n_batch, H_A, t_eff, KV_LORA), BF16),
                   jax.ShapeDtypeStruct((n_batch, H_A, t_eff, ROPE_DIM), BF16)],
        compiler_params=_cparams(("parallel",)),
        name="qpath",
    )(qn, wuq, wukt, tqr)
    return dict(ckv=ckv, kr=kr, kb=kb, vb=vb, ki=ki, wi=wi, ckv_b=ckv_b, kr_b=kr_b, kb_b=kb_b, vb_b=vb_b,
                ki_b=ki_b, qb=qb, qi=qi, gates=gates, ql=ql, qr=qr)


def _mla_prompt_kernel(ql_ref, qr_ref, c_ref, kr_ref, o_ref, m_sc, l_sc, acc_sc, *, tq, tk):
    i, j = pl.program_id(1), pl.program_id(2)
    rows = H_A * tq

    @pl.when(j == 0)
    def _():
        m_sc[...] = jnp.full_like(m_sc, -jnp.inf)
        l_sc[...] = jnp.zeros_like(l_sc)
        acc_sc[...] = jnp.zeros_like(acc_sc)

    @pl.when(j * tk < (i + 1) * tq)
    def _():
        ql = ql_ref[0].reshape(rows, KV_LORA)
        qr = qr_ref[0].reshape(rows, ROPE_DIM)
        c = c_ref[0]
        s = (lax.dot_general(ql, c, NT_DIMS, preferred_element_type=F32)
             + lax.dot_general(qr, kr_ref[0], NT_DIMS, preferred_element_type=F32)) * MLA_SCALE
        q_pos = i * tq + lax.broadcasted_iota(I32, (H_A, tq, tk), 1).reshape(rows, tk)
        k_pos = j * tk + lax.broadcasted_iota(I32, (rows, tk), 1)
        s = jnp.where(k_pos <= q_pos, s, -jnp.inf)
        m_new = jnp.maximum(m_sc[...], jnp.max(s, axis=1, keepdims=True))
        a = jnp.exp(m_sc[...] - m_new)
        p = jnp.exp(s - m_new)
        l_sc[...] = a * l_sc[...] + jnp.sum(p, axis=1, keepdims=True)
        acc_sc[...] = a * acc_sc[...] + jnp.dot(p.astype(BF16), c, preferred_element_type=F32)
        m_sc[...] = m_new

    @pl.when(j == pl.num_programs(2) - 1)
    def _():
        o_ref[0] = (acc_sc[...] / l_sc[...]).astype(BF16).reshape(H_A, tq, KV_LORA)


def _mla_prompt(ql, qr, ckv_b, kr_b):
    b, _, t, _ = ql.shape
    tq = min(256, t)
    tk = min(512, t)
    last = lambda i, j: jnp.minimum(j, ((i + 1) * tq - 1) // tk)
    return pl.pallas_call(
        functools.partial(_mla_prompt_kernel, tq=tq, tk=tk),
        grid=(b, t // tq, t // tk),
        in_specs=[pl.BlockSpec((1, H_A, tq, KV_LORA), lambda bb, i, j: (bb, 0, i, 0)),
                  pl.BlockSpec((1, H_A, tq, ROPE_DIM), lambda bb, i, j: (bb, 0, i, 0)),
                  pl.BlockSpec((1, tk, KV_LORA), lambda bb, i, j: (bb, last(i, j), 0)),
                  pl.BlockSpec((1, tk, ROPE_DIM), lambda bb, i, j: (bb, last(i, j), 0))],
        out_specs=pl.BlockSpec((1, H_A, tq, KV_LORA), lambda bb, i, j: (bb, 0, i, 0)),
        out_shape=jax.ShapeDtypeStruct((b, H_A, t, KV_LORA), BF16),
        scratch_shapes=[pltpu.VMEM((H_A * tq, 1), F32), pltpu.VMEM((H_A * tq, 1), F32),
                        pltpu.VMEM((H_A * tq, KV_LORA), F32)],
        compiler_params=_cparams(("parallel", "parallel", "arbitrary")),
        name="mla_prompt",
    )(ql, qr, ckv_b, kr_b)


DSA_KEY_SPAN = 512


def _dsa_prompt_kernel(qi_ref, wi_ref, qb_ref, ki_ref, kb_ref, vb_ref, o_ref, *, tq, k_sel):
    i = pl.program_id(1)
    t_full = ki_ref.shape[1]
    span = min(DSA_KEY_SPAN, t_full)
    tiles_per_span = span // tq
    for v in range(t_full // span):
        pl.when(i // tiles_per_span == v)(
            functools.partial(_dsa_prompt_body, qi_ref, wi_ref, qb_ref, ki_ref, kb_ref, vb_ref, o_ref,
                              tq=tq, k_sel=k_sel, t_all=(v + 1) * span))


def _dsa_prompt_body(qi_ref, wi_ref, qb_ref, ki_ref, kb_ref, vb_ref, o_ref, *, tq, k_sel, t_all):
    i = pl.program_id(1)
    ki = ki_ref[0, :t_all, :]
    wi = wi_ref[...]
    score = jnp.zeros((tq, t_all), F32)
    for h in range(H_IDX):
        d = lax.dot_general(qi_ref[:, h * IDX_DIM:(h + 1) * IDX_DIM], ki, NT_DIMS, preferred_element_type=F32)
        score = score + wi[:, h:h + 1] * jnp.maximum(d, 0.0)
    q_pos = i * tq + lax.broadcasted_iota(I32, (tq, t_all), 0)
    k_pos = lax.broadcasted_iota(I32, (tq, t_all), 1)
    causal = k_pos <= q_pos
    key = jnp.where(causal, _sort_key(score), INT_MIN)
    thr = _kth_largest_key(key, k_sel)
    keep = (causal & (key >= thr))[None]
    hpg = H_B // KV_HEADS_B
    for g in range(KV_HEADS_B):
        qg = jnp.concatenate([qb_ref[:, (g * hpg + jj) * HD_B:(g * hpg + jj + 1) * HD_B] for jj in range(hpg)], axis=0)
        kg = kb_ref[0, :t_all, g * HD_B:(g + 1) * HD_B]
        vg = vb_ref[0, :t_all, g * HD_B:(g + 1) * HD_B]
        s = lax.dot_general(qg, kg, NT_DIMS, preferred_element_type=F32) * DSA_SCALE
        s = jnp.where(keep, s.reshape(hpg, tq, t_all), -jnp.inf).reshape(hpg * tq, t_all)
        p = jnp.exp(s - jnp.max(s, axis=1, keepdims=True))
        l = jnp.sum(p, axis=1, keepdims=True)
        o = (jnp.dot(p.astype(BF16), vg, preferred_element_type=F32) / l).astype(BF16)
        for jj in range(hpg):
            o_ref[:, (g * hpg + jj) * HD_B:(g * hpg + jj + 1) * HD_B] = o[jj * tq:(jj + 1) * tq]


def _dsa_prompt(qi, wi, qb, ki_b, kb_b, vb_b, b, t):
    tq = min(128, t)
    k_sel = min(TOPK_MAX, t // 4)
    tpb = t // tq
    row = lambda w: pl.BlockSpec((tq, w), lambda bb, i: (bb * tpb + i, 0))
    per_b = lambda w: pl.BlockSpec((1, t, w), lambda bb, i: (bb, 0, 0))
    kvw = KV_HEADS_B * HD_B
    return pl.pallas_call(
        functools.partial(_dsa_prompt_kernel, tq=tq, k_sel=k_sel),
        grid=(b, tpb),
        in_specs=[row(H_IDX * IDX_DIM), row(H_IDX), row(H_B * HD_B), per_b(IDX_DIM), per_b(kvw), per_b(kvw)],
        out_specs=row(H_B * HD_B),
        out_shape=jax.ShapeDtypeStruct((b * t, H_B * HD_B), BF16),
        compiler_params=_cparams(("parallel", "arbitrary")),
        name="dsa_prompt",
    )(qi, wi, qb, ki_b.reshape(b, t, IDX_DIM), kb_b.reshape(b, t, kvw), vb_b.reshape(b, t, kvw))


def _merge_kernel(ol_ref, ob_ref, g_ref, x_ref, wuv_ref, wao_ref, wbo_ref, wout_ref, lg_ref, lb_ref, h_ref):
    ta = jnp.concatenate([jnp.dot(ol_ref[0, h], wuv_ref[h], preferred_element_type=F32) for h in range(H_A)],
                         axis=1).astype(BF16)
    ya = jnp.dot(ta, wao_ref[...], preferred_element_type=F32)
    yb = jnp.dot(ob_ref[...], wbo_ref[...], preferred_element_type=F32)
    m = (g_ref[:, :D_MODEL] * ya + g_ref[:, D_MODEL:] * yb).astype(BF16)
    mix = jnp.dot(m, wout_ref[...], preferred_element_type=F32)
    h_ref[...] = _layer_norm(ALPHA * x_ref[...] + mix, lg_ref[...], lb_ref[...])


def _merge(o_lat, o_b, gates, x2, weights):
    b, _, t, _ = o_lat.shape
    tm = min(256, t)
    tpb = t // tm
    wuv, wao, wbo, wout, lg, lb = weights
    row = lambda w: pl.BlockSpec((tm, w), lambda i: (i, 0))
    once = lambda a: pl.BlockSpec(a.shape, lambda i: (0,) * a.ndim, pipeline_mode=pl.Buffered(1))
    return pl.pallas_call(
        _merge_kernel,
        grid=(b * tpb,),
        in_specs=[pl.BlockSpec((1, H_A, tm, KV_LORA), lambda i: (i // tpb, 0, i % tpb, 0)),
                  row(H_B * HD_B), row(2 * D_MODEL), row(D_MODEL),
                  once(wuv), once(wao), once(wbo), once(wout), once(lg), once(lb)],
        out_specs=row(D_MODEL),
        out_shape=jax.ShapeDtypeStruct((b * t, D_MODEL), F32),
        compiler_params=_cparams(("parallel",)),
        name="merge",
    )(o_lat, o_b, gates, x2, wuv, wao, wbo, wout, lg, lb)


def _peer_route_kernel(h_ref, wq_ref, sk_ref, e_ref, g_ref):
    q = jnp.dot(h_ref[...].astype(BF16), wq_ref[...], preferred_element_type=F32).astype(BF16)
    half = PEER_DQ // 2
    gates, experts = [], []
    for h in range(PEER_HEADS):
        sv, si = [], []
        for c in range(2):
            qhc = q[:, (2 * h + c) * half:(2 * h + c + 1) * half]
            s_t = lax.dot_general(sk_ref[h, c], qhc, NT_DIMS, preferred_element_type=F32)
            v, ix = _top_rows(s_t, PEER_TOPK)
            sv.append(v)
            si.append(ix)
        kk = PEER_TOPK
        tm = sv[0].shape[1]
        a8 = lax.broadcasted_iota(I32, (8, tm), 0)
        a16 = lax.broadcasted_iota(I32, (kk, tm), 0)
        cand = [sv[0] + sv[1][0:1]]
        cidx = [si[0] * float(N_KEYS) + si[1][0:1]]
        flat = [a16 * kk]
        for b in range(1, 8):
            ok = (a8 + 1) * (b + 1) <= kk
            cand.append(jnp.where(ok, sv[0][0:8] + sv[1][b:b + 1], -jnp.inf))
            cidx.append(si[0][0:8] * float(N_KEYS) + si[1][b:b + 1])
            flat.append(a8 * kk + b)
        cand.append(sv[0][0:1] + sv[1][8:kk])
        cidx.append(si[0][0:1] * float(N_KEYS) + si[1][8:kk])
        flat.append(a8 + 8)
        cand = jnp.concatenate(cand, axis=0)
        cidx = jnp.concatenate(cidx, axis=0)
        rid = jnp.concatenate(flat, axis=0).astype(F32)
        fv, fe = [], []
        for _ in range(PEER_TOPK):
            m = jnp.max(cand, axis=0, keepdims=True)
            pos = jnp.min(jnp.where(cand == m, rid, float(kk * kk)), axis=0, keepdims=True)
            hit = rid == pos
            fe.append(jnp.sum(jnp.where(hit, cidx, 0.0), axis=0, keepdims=True))
            cand = jnp.where(hit, -jnp.inf, cand)
            fv.append(m)
        fv = jnp.concatenate(fv, axis=0)
        p = jnp.exp(fv - fv[0:1])
        gates.append(p / jnp.sum(p, axis=0, keepdims=True))
        experts.append(jnp.concatenate(fe, axis=0))
    g_ref[...] = jnp.concatenate(gates, axis=0).T
    e_ref[...] = jnp.concatenate(experts, axis=0).T.astype(I32)


def _peer_route(h2, wq, sk):
    n = h2.shape[0]
    tm = min(128, n)
    once = lambda a: pl.BlockSpec(a.shape, lambda i: (0,) * a.ndim)
    return pl.pallas_call(
        _peer_route_kernel,
        grid=(n // tm,),
        in_specs=[pl.BlockSpec((tm, D_MODEL), lambda i: (i, 0)), once(wq), once(sk)],
        out_specs=[pl.BlockSpec((tm, N_SEL), lambda i: (i, 0)), pl.BlockSpec((tm, N_SEL), lambda i: (i, 0))],
        out_shape=[jax.ShapeDtypeStruct((n, N_SEL), I32), jax.ShapeDtypeStruct((n, N_SEL), F32)],
        compiler_params=_cparams(("parallel",)),
        name="peer_route",
    )(h2, wq, sk)


N_SEL = PEER_HEADS * PEER_TOPK
PEER_TT = 32
PEER_SLOTS = 4
PEER_AHEAD = PEER_SLOTS - 1
IDX_RING = 3
PEER_CHUNK = 16
HALF_D = D_MODEL // 2


def _pack_bf16_pairs(a):
    b = lax.bitcast_convert_type(a.astype(BF16), jnp.uint16).astype(jnp.uint32)
    h = a.shape[1] // 2
    return lax.bitcast_convert_type(b[:, :h] | (b[:, h:] << 16), I32)


def _low_bf16(w):
    return pltpu.bitcast(w << 16, F32)


def _high_bf16(w):
    return pltpu.bitcast(w & -65536, F32)


def _peer_expert_kernel(e_hbm, h_ref, g_ref, tab_hbm, lg_ref, lb_ref, y_ref,
                        idx_sm, *rest, n_total, n_steps):
    buf, (po_sc, sem_i, sem_g) = rest[:PEER_SLOTS], rest[PEER_SLOTS:]
    i = pl.program_id(0)
    tt = PEER_TT

    def idx_copy(step):
        ring = step % IDX_RING
        return pltpu.make_async_copy(e_hbm.at[step], idx_sm.at[pl.ds(ring * tt, tt)], sem_i.at[ring])

    def row_copy(e, slot, k):
        return pltpu.make_async_copy(tab_hbm.at[e], buf[slot].at[pl.ds(k, 1)], sem_g.at[slot])

    def idx_row(n):
        n = jnp.minimum(n, n_total - 1)
        return ((n // tt) % IDX_RING) * tt + n % tt

    def issue(row, slot, k0, k1):
        for k in range(k0, k1):
            row_copy(idx_sm[row, k], slot, k).start(priority=k % 2)

    def wait(slot):
        other = buf[(slot + 1) % PEER_SLOTS]
        pltpu.make_async_copy(other, buf[slot], sem_g.at[slot]).wait()

    @pl.when(i == 0)
    def _():
        idx_copy(0).start()
        if n_steps > 1:
            idx_copy(1).start()
        idx_copy(0).wait()
        for t in range(PEER_AHEAD):
            issue(idx_row(t), t, 0, N_SEL)

    @pl.when(i + 1 < n_steps)
    def _():
        idx_copy(i + 1).wait()

    @pl.when(i + 2 < n_steps)
    def _():
        idx_copy(i + 2).start()

    n_chunks = N_SEL // PEER_CHUNK
    per_chunk = N_SEL // (2 * n_chunks)

    def token(t, slot):
        wait(slot)
        row = idx_row(i * tt + t + PEER_AHEAD)
        ahead_slot = (slot + PEER_AHEAD) % PEER_SLOTS
        x = h_ref[pl.ds(t, 1), :]
        x_lo, x_hi = x[:, :HALF_D], x[:, HALF_D:]
        words = buf[slot]
        hk = []
        for c in range(n_chunks):
            wu = words[c * PEER_CHUNK:(c + 1) * PEER_CHUNK, :HALF_D]
            hk.append(jnp.sum(_low_bf16(wu) * x_lo + _high_bf16(wu) * x_hi, axis=1, keepdims=True))
            issue(row, ahead_slot, c * per_chunk, (c + 1) * per_chunk)
        hk = jnp.concatenate(hk, axis=0)
        gcol = jnp.transpose(jnp.broadcast_to(g_ref[pl.ds(t, 1), :], (N_SEL, N_SEL)))[:, 0:1]
        a = gcol * (0.5 * hk * (1.0 + lax.erf(hk * math.sqrt(0.5))))
        acc_lo = jnp.zeros((8, HALF_D), F32)
        acc_hi = jnp.zeros((8, HALF_D), F32)
        for c in range(n_chunks):
            wv = words[c * PEER_CHUNK:(c + 1) * PEER_CHUNK, HALF_D:]
            ac = a[c * PEER_CHUNK:(c + 1) * PEER_CHUNK]
            for r in range(0, PEER_CHUNK, 8):
                acc_lo = acc_lo + _low_bf16(wv[r:r + 8]) * ac[r:r + 8]
                acc_hi = acc_hi + _high_bf16(wv[r:r + 8]) * ac[r:r + 8]
            issue(row, ahead_slot, (n_chunks + c) * per_chunk, (n_chunks + c + 1) * per_chunk)
        po_sc[pl.ds(t, 1), :] = jnp.concatenate(
            [jnp.sum(acc_lo, axis=0, keepdims=True), jnp.sum(acc_hi, axis=0, keepdims=True)], axis=1)

    def group(g, carry):
        for j in range(PEER_SLOTS):
            token(g * PEER_SLOTS + j, j)
        return carry

    lax.fori_loop(0, tt // PEER_SLOTS, group, 0)

    @pl.when(i == n_steps - 1)
    def _():
        for j in range(PEER_AHEAD):
            wait(j)

    y_ref[...] = _layer_norm(ALPHA * h_ref[...] + po_sc[...], lg_ref[...], lb_ref[...])


def _peer_experts(eidx, gate, h2, table, lg, lb):
    n = h2.shape[0]
    tt = PEER_TT
    assert n % tt == 0 and tt % PEER_SLOTS == 0
    steps = n // tt
    row = lambda w: pl.BlockSpec((tt, w), lambda i: (i, 0))
    anyspec = pl.BlockSpec(memory_space=pl.ANY)
    return pl.pallas_call(
        functools.partial(_peer_expert_kernel, n_total=n, n_steps=steps),
        grid=(steps,),
        in_specs=[anyspec, row(D_MODEL), row(N_SEL), anyspec,
                  _const_spec((1, D_MODEL)), _const_spec((1, D_MODEL))],
        out_specs=row(D_MODEL),
        out_shape=jax.ShapeDtypeStruct((n, D_MODEL), F32),
        scratch_shapes=[pltpu.SMEM((IDX_RING * tt, N_SEL), I32),
                        *[pltpu.VMEM((N_SEL, D_MODEL), I32) for _ in range(PEER_SLOTS)],
                        pltpu.VMEM((tt, D_MODEL), F32),
                        pltpu.SemaphoreType.DMA((IDX_RING,)),
                        pltpu.SemaphoreType.DMA((PEER_SLOTS,))],
        compiler_params=_cparams(("arbitrary",)),
        name="peer_experts",
    )(eidx.reshape(steps, tt, N_SEL), h2, gate, table, lg, lb)


def _peer_layer(h2, peer_w):
    wq, sk, table, lg, lb = peer_w
    eidx, gate = _peer_route(h2, wq, sk)
    return _peer_experts(eidx, gate, h2, table, lg, lb)


PAGES_PER_STEP = 16


def _stream_pages(pt_ref, pools, bufs, sem, page_window):
    pp = PAGES_PER_STEP
    nps = pl.num_programs(1)
    step = pl.program_id(0) * nps + pl.program_id(1)
    total = pl.num_programs(0) * nps
    slot = step % 2

    def start(s, into):
        seq, group = s // nps, s % nps
        for j in range(pp):
            page = pt_ref[seq, group * pp + j]
            for a, pool in enumerate(pools):
                pltpu.make_async_copy(pool.at[page], page_window(a, bufs[a].at[into], j), sem.at[a, into]).start()

    @pl.when(step == 0)
    def _():
        start(step, slot)

    @pl.when(step + 1 < total)
    def _():
        start(step + 1, 1 - slot)

    for a in range(len(pools)):
        pltpu.make_async_copy(bufs[a].at[1 - slot], bufs[a].at[slot], sem.at[a, slot]).wait()
    return slot


def _rows_window(rows_per_page):
    return lambda buf, j: buf.at[pl.ds(j * rows_per_page, rows_per_page)]


def _lanes_window(buf, j):
    return buf.at[:, pl.ds(j * PAGE_SIZE, PAGE_SIZE)]


def _sample_pass1_kernel(pt_ref, ql_ref, qr_ref, qi_ref, wi_ref, cn_ref, krn_ref, c_hbm, kr_hbm, ki_hbm,
                         o_ref, sc_ref, m_sc, l_sc, acc_sc, cbuf, krbuf, kibuf, sem):
    p_id = pl.program_id(1)
    windows = (_rows_window(PAGE_SIZE), _lanes_window, _lanes_window)
    slot = _stream_pages(pt_ref, (c_hbm, kr_hbm, ki_hbm), (cbuf, krbuf, kibuf), sem,
                         lambda a, buf, j: windows[a](buf, j))

    @pl.when(p_id == 0)
    def _():
        m_sc[...] = jnp.full_like(m_sc, -jnp.inf)
        l_sc[...] = jnp.zeros_like(l_sc)
        acc_sc[...] = jnp.zeros_like(acc_sc)

    ql, qr = ql_ref[0], qr_ref[0]

    def absorb(s, weighted_values):
        s = s * MLA_SCALE
        m_new = jnp.maximum(m_sc[...], jnp.max(s, axis=1, keepdims=True))
        a = jnp.exp(m_sc[...] - m_new)
        p = jnp.exp(s - m_new)
        l_sc[...] = a * l_sc[...] + jnp.sum(p, axis=1, keepdims=True)
        acc_sc[...] = a * acc_sc[...] + weighted_values(p.astype(BF16))
        m_sc[...] = m_new

    c = cbuf[slot].astype(BF16)
    kr_t = krbuf[slot].astype(BF16)
    absorb(lax.dot_general(ql, c, NT_DIMS, preferred_element_type=F32)
           + jnp.dot(qr, kr_t, preferred_element_type=F32),
           lambda p: jnp.dot(p, c, preferred_element_type=F32))
    ki_t = kibuf[slot].astype(BF16)
    d = jnp.dot(qi_ref[0], ki_t, preferred_element_type=F32)
    sc_ref[0] = jnp.sum(wi_ref[0] * jnp.maximum(d, 0.0), axis=0, keepdims=True)

    @pl.when(p_id == pl.num_programs(1) - 1)
    def _():
        cn = cn_ref[0].astype(F32)
        s_new = (jnp.sum(ql.astype(F32) * cn, axis=1, keepdims=True)
                 + jnp.sum(qr.astype(F32) * krn_ref[0].astype(F32), axis=1, keepdims=True))
        absorb(s_new, lambda p: p.astype(F32) * cn)
        o_ref[0] = (acc_sc[...] / l_sc[...]).astype(BF16)


def _sample_pass1(page_table, ql, qr, qi3, wi3, ckv_new, kr_new, c_pool, kr_pool, ki_pool):
    db, n_pages = page_table.shape
    pp = PAGES_PER_STEP
    per_b = lambda a: pl.BlockSpec((1,) + a.shape[1:], lambda b, p, pt: (b,) + (0,) * (a.ndim - 1))
    anyspec = pl.BlockSpec(memory_space=pl.ANY)
    span = pp * PAGE_SIZE
    grid_spec = pltpu.PrefetchScalarGridSpec(
        num_scalar_prefetch=1,
        grid=(db, n_pages // pp),
        in_specs=[per_b(ql), per_b(qr), per_b(qi3), per_b(wi3), per_b(ckv_new), per_b(kr_new),
                  anyspec, anyspec, anyspec],
        out_specs=[pl.BlockSpec((1, H_A, KV_LORA), lambda b, p, pt: (b, 0, 0)),
                   pl.BlockSpec((1, 1, span), lambda b, p, pt: (b, 0, p))],
        scratch_shapes=[pltpu.VMEM((H_A, 1), F32), pltpu.VMEM((H_A, 1), F32), pltpu.VMEM((H_A, KV_LORA), F32),
                        pltpu.VMEM((2, span, KV_LORA), F32), pltpu.VMEM((2, ROPE_DIM, span), F32),
                        pltpu.VMEM((2, IDX_DIM, span), F32), pltpu.SemaphoreType.DMA((3, 2))],
    )
    return pl.pallas_call(
        _sample_pass1_kernel,
        grid_spec=grid_spec,
        out_shape=[jax.ShapeDtypeStruct((db, H_A, KV_LORA), BF16),
                   jax.ShapeDtypeStruct((db, 1, n_pages * PAGE_SIZE), F32)],
        compiler_params=_cparams(("arbitrary", "arbitrary")),
        name="sample_pass1",
    )(page_table, ql, qr, qi3, wi3, ckv_new, kr_new, c_pool, kr_pool, ki_pool)


def _sample_select_kernel(sc_ref, qi_ref, wi_ref, kin_ref, thr_ref, new_ref, *, k_sel):
    past = sc_ref[...]
    qi = qi_ref[...].astype(F32)
    kn = kin_ref[...].astype(F32)
    d = jnp.sum(qi * kn, axis=2)
    s_new = jnp.sum(wi_ref[...] * jnp.maximum(d, 0.0), axis=1, keepdims=True)
    key_new = _sort_key(s_new)
    lane0 = lax.broadcasted_iota(I32, (past.shape[0], LANE), 1) == 0
    key = jnp.concatenate([_sort_key(past), jnp.where(lane0, key_new, INT_MIN)], axis=1)
    thr = _kth_largest_key(key, k_sel)
    thr_ref[...] = thr
    new_ref[...] = (key_new >= thr).astype(I32)


def _sample_select(scores, qi3, wi2, ki_new3, k_sel):
    db, past = scores.shape
    rows = min(8, db)
    return pl.pallas_call(
        functools.partial(_sample_select_kernel, k_sel=k_sel),
        grid=(db // rows,),
        in_specs=[pl.BlockSpec((rows, past), lambda i: (i, 0)),
                  pl.BlockSpec((rows, H_IDX, IDX_DIM), lambda i: (i, 0, 0)),
                  pl.BlockSpec((rows, H_IDX), lambda i: (i, 0)),
                  pl.BlockSpec((rows, 1, IDX_DIM), lambda i: (i, 0, 0))],
        out_specs=[pl.BlockSpec((rows, 1), lambda i: (i, 0)), pl.BlockSpec((rows, 1), lambda i: (i, 0))],
        out_shape=[jax.ShapeDtypeStruct((db, 1), I32), jax.ShapeDtypeStruct((db, 1), I32)],
        compiler_params=_cparams(("parallel",)),
        name="sample_select",
    )(scores, qi3, wi2, ki_new3)


def _sample_pass2_kernel(pt_ref, thr_ref, new_ref, qb_ref, sc_ref, kn_ref, vn_ref, k_hbm, v_hbm,
                         o_ref, m_sc, l_sc, acc_sc, kbuf, vbuf, sem):
    pp = PAGES_PER_STEP
    b, p_id = pl.program_id(0), pl.program_id(1)
    hpg = H_B // KV_HEADS_B
    window = _rows_window(PAGE_SIZE * KV_HEADS_B)
    slot = _stream_pages(pt_ref, (k_hbm, v_hbm), (kbuf, vbuf), sem, lambda a, buf, j: window(buf, j))

    @pl.when(p_id == 0)
    def _():
        m_sc[...] = jnp.full_like(m_sc, -jnp.inf)
        l_sc[...] = jnp.zeros_like(l_sc)
        acc_sc[...] = jnp.zeros_like(acc_sc)

    def absorb(g, s, keep, weighted_values):
        rows = slice(g * hpg, (g + 1) * hpg)
        s = jnp.where(keep, s * DSA_SCALE, -jnp.inf)
        m_old = m_sc[rows]
        m_new = jnp.maximum(m_old, jnp.max(s, axis=1, keepdims=True))
        m_safe = jnp.where(m_new == -jnp.inf, 0.0, m_new)
        a = jnp.exp(m_old - m_safe)
        p = jnp.exp(s - m_safe)
        l_sc[rows] = a * l_sc[rows] + jnp.sum(p, axis=1, keepdims=True)
        acc_sc[rows] = a * acc_sc[rows] + weighted_values(p.astype(BF16))
        m_sc[rows] = m_new

    keep = _sort_key(sc_ref[0]) >= thr_ref[b]
    for g in range(KV_HEADS_B):
        own = pl.ds(g, pp * PAGE_SIZE, stride=KV_HEADS_B)
        k = kbuf[slot, own, :].astype(BF16)
        v = vbuf[slot, own, :].astype(BF16)
        absorb(g, lax.dot_general(qb_ref[0, g * hpg:(g + 1) * hpg], k, NT_DIMS, preferred_element_type=F32),
               keep, lambda p, v=v: jnp.dot(p, v, preferred_element_type=F32))

    @pl.when(p_id == pl.num_programs(1) - 1)
    def _():
        for g in range(KV_HEADS_B):
            cols = slice(g * HD_B, (g + 1) * HD_B)
            kn = kn_ref[0, :, cols].astype(F32)
            vn = vn_ref[0, :, cols].astype(F32)
            s_new = jnp.sum(qb_ref[0, g * hpg:(g + 1) * hpg].astype(F32) * kn, axis=1, keepdims=True)
            absorb(g, s_new, new_ref[b] > 0, lambda p, vn=vn: p.astype(F32) * vn)
        o_ref[0] = (acc_sc[...] / l_sc[...]).astype(BF16)


def _sample_pass2(page_table, thr, sel_new, qb3, scores3, kb_new, vb_new, k_pool, v_pool):
    db, n_pages = page_table.shape
    pp = PAGES_PER_STEP
    kvw = KV_HEADS_B * HD_B
    per_b = lambda a: pl.BlockSpec((1,) + a.shape[1:], lambda b, p, *_: (b,) + (0,) * (a.ndim - 1))
    anyspec = pl.BlockSpec(memory_space=pl.ANY)
    group_rows = pp * PAGE_SIZE * KV_HEADS_B
    grid_spec = pltpu.PrefetchScalarGridSpec(
        num_scalar_prefetch=3,
        grid=(db, n_pages // pp),
        in_specs=[per_b(qb3), pl.BlockSpec((1, 1, pp * PAGE_SIZE), lambda b, p, *_: (b, 0, p)),
                  per_b(kb_new), per_b(vb_new), anyspec, anyspec],
        out_specs=pl.BlockSpec((1, H_B, HD_B), lambda b, p, *_: (b, 0, 0)),
        scratch_shapes=[pltpu.VMEM((H_B, 1), F32), pltpu.VMEM((H_B, 1), F32), pltpu.VMEM((H_B, HD_B), F32),
                        pltpu.VMEM((2, group_rows, HD_B), F32), pltpu.VMEM((2, group_rows, HD_B), F32),
                        pltpu.SemaphoreType.DMA((2, 2))],
    )
    return pl.pallas_call(
        _sample_pass2_kernel,
        grid_spec=grid_spec,
        out_shape=jax.ShapeDtypeStruct((db, H_B, HD_B), BF16),
        compiler_params=_cparams(("arbitrary", "arbitrary")),
        name="sample_pass2",
    )(page_table, thr, sel_new, qb3, scores3, kb_new, vb_new, k_pool, v_pool)


def kernel(x_prompt, x_sample, cache_kv_latent, cache_k_rope, cache_k, cache_v, cache_idx_k, page_table, w_in, mla_q_norm, mla_w_uq, mla_kv_norm, mla_w_uk, mla_w_uv, mla_w_o, idx_k_norm_g, idx_k_norm_b, dsa_w_o, w_out, ln1_g, ln1_b, peer_w_q, peer_sub_keys, peer_u, peer_v, ln2_g, ln2_b):
    b, t, _ = x_prompt.shape
    db, ts, _ = x_sample.shape
    assert ts == 1, "the sample group decodes one token per sequence"
    n_pages = page_table.shape[1]
    past_len = n_pages * PAGE_SIZE
    l = 0

    wq_, wkv_, wbq_, wbk_, wbv_, wiq_, wik_, wiw_, wg_ = jnp.split(w_in[l], IN_OFFSETS, axis=1)
    padc = lambda a: jnp.pad(a, ((0, 0), (0, LANE - a.shape[1])))
    w1 = jnp.concatenate([wq_, wkv_[:, :KV_LORA], padc(wkv_[:, KV_LORA:]), wbk_, wbv_, padc(wik_), padc(wiw_)],
                         axis=1).astype(BF16)
    w2 = jnp.concatenate([wbq_, wiq_], axis=1).astype(BF16)
    wg = wg_.astype(BF16)
    wuq = jnp.concatenate([mla_w_uq[l][:, :, :NOPE_DIM].reshape(Q_LORA, H_A * NOPE_DIM),
                           mla_w_uq[l][:, :, NOPE_DIM:].reshape(Q_LORA, H_A * ROPE_DIM)], axis=1).astype(BF16)
    wukt = jnp.transpose(mla_w_uk[l], (1, 2, 0)).astype(BF16)
    row1 = lambda a: a.reshape(1, -1).astype(F32)
    pad1 = lambda a: jnp.pad(a.astype(F32), (0, LANE - a.shape[0])).reshape(1, LANE)
    proj_w = (w1, w2, wg, row1(mla_q_norm[l]), row1(mla_kv_norm[l]), pad1(idx_k_norm_g[l]), pad1(idx_k_norm_b[l]),
              wuq, wukt)
    merge_w = (jnp.transpose(mla_w_uv[l], (1, 0, 2)).astype(BF16), mla_w_o[l].astype(BF16),
               dsa_w_o[l].astype(BF16), w_out[l].astype(BF16), row1(ln1_g[l]), row1(ln1_b[l]))
    peer_table = jnp.concatenate([_pack_bf16_pairs(peer_u[l]), _pack_bf16_pairs(peer_v[l])], axis=1)
    peer_table = peer_table.reshape(peer_table.shape[0], 1, D_MODEL)
    peer_w = (peer_w_q[l].astype(BF16), peer_sub_keys[l].astype(BF16), peer_table, row1(ln2_g[l]), row1(ln2_b[l]))
    kvw = KV_HEADS_B * HD_B

    xs2 = x_sample.reshape(db, D_MODEL)
    sp = _project(xs2, jnp.full((1,), past_len, I32), 1, proj_w)
    ql_s = jnp.transpose(sp["ql"][0], (1, 0, 2))
    qr_s = jnp.transpose(sp["qr"][0], (1, 0, 2))
    qi3 = sp["qi"].reshape(db, H_IDX, IDX_DIM)
    o_lat_s, scores = _sample_pass1(
        page_table, ql_s, qr_s, qi3, sp["wi"].reshape(db, H_IDX, 1),
        sp["ckv_b"].reshape(db, 1, KV_LORA), sp["kr_b"].reshape(db, 1, ROPE_DIM),
        cache_kv_latent[l], jnp.swapaxes(cache_k_rope[l], 1, 2), jnp.swapaxes(cache_idx_k[l], 1, 2))
    k_sel_s = min(TOPK_MAX, (past_len + 1) // 4)
    thr, sel_new = _sample_select(scores.reshape(db, past_len), qi3, sp["wi"], sp["ki_b"].reshape(db, 1, IDX_DIM),
                                  k_sel_s)
    o_b_s = _sample_pass2(page_table, thr.reshape(db), sel_new.reshape(db), sp["qb"].reshape(db, H_B, HD_B), scores,
                          sp["kb_b"].reshape(db, 1, kvw), sp["vb_b"].reshape(db, 1, kvw),
                          cache_k[l].reshape(-1, PAGE_SIZE * KV_HEADS_B, HD_B),
                          cache_v[l].reshape(-1, PAGE_SIZE * KV_HEADS_B, HD_B))
    o_lat_s4 = jnp.transpose(o_lat_s, (1, 0, 2)).reshape(1, H_A, db, KV_LORA)
    h_s = _merge(o_lat_s4, o_b_s.reshape(db, H_B * HD_B), sp["gates"], xs2, merge_w)
    y_s = _peer_layer(h_s, peer_w).reshape(db, 1, D_MODEL)

    xp2 = x_prompt.reshape(b * t, D_MODEL)
    pp_ = _project(xp2, jnp.arange(t, dtype=I32), b, proj_w)
    o_lat_p = _mla_prompt(pp_["ql"], pp_["qr"], pp_["ckv_b"].reshape(b, t, KV_LORA), pp_["kr_b"].reshape(b, t, ROPE_DIM))
    o_b_p = _dsa_prompt(pp_["qi"], pp_["wi"], pp_["qb"], pp_["ki_b"], pp_["kb_b"], pp_["vb_b"], b, t)
    h_p = _merge(o_lat_p, o_b_p, pp_["gates"], xp2, merge_w)
    y_p = _peer_layer(h_p, peer_w).reshape(b, t, D_MODEL)

    def rows(d, bb, tt_):
        return (d["ckv"].reshape(1, bb, tt_, KV_LORA), d["kr"].reshape(1, bb, tt_, ROPE_DIM),
                d["kb"].reshape(1, bb, tt_, KV_HEADS_B, HD_B), d["vb"].reshape(1, bb, tt_, KV_HEADS_B, HD_B),
                d["ki"].reshape(1, bb, tt_, IDX_DIM))

    return (y_p, y_s) + rows(pp_, b, t) + rows(sp, db, 1)
```

```python
import functools
import math

import numpy as np
import jax
import jax.numpy as jnp
from jax import lax
from jax.experimental import pallas as pl
from jax.experimental.pallas import tpu as pltpu

F32 = jnp.float32
BF16 = jnp.bfloat16
I32 = jnp.int32

D_MODEL = 2048
PAGE_SIZE = 128
H_A = 8
Q_LORA = 512
KV_LORA = 512
NOPE_DIM = 128
ROPE_DIM = 64
V_DIM = 128
MLA_SCALE = (NOPE_DIM + ROPE_DIM) ** -0.5
H_B = 8
HD_B = 128
KV_HEADS_B = 2
ROT_B = HD_B // 4
DSA_SCALE = HD_B ** -0.5
H_IDX = 16
IDX_DIM = 64
ROT_IDX = IDX_DIM // 4
IDX_SCALE = IDX_DIM ** -0.5
W_IDX_SCALE = H_IDX ** -0.5
TOPK_MAX = 256
PEER_HEADS = 8
N_KEYS = 128
PEER_DQ = 256
PEER_TOPK = 16
ROPE_THETA = 500000.0
RMS_EPS = 1e-6
LN_EPS = 1e-5
DEPTH = 1
ALPHA = (2 * DEPTH) ** 0.25

IN_SIZES = (Q_LORA, KV_LORA + ROPE_DIM, H_B * HD_B, KV_HEADS_B * HD_B, KV_HEADS_B * HD_B,
            H_IDX * IDX_DIM, IDX_DIM, H_IDX, 2 * D_MODEL)
IN_OFFSETS = tuple(int(o) for o in np.cumsum(IN_SIZES)[:-1])

LANE = 128
INT_MIN = -2 ** 31
VMEM_LIMIT = 56 * 1024 * 1024

O_Q = 0
O_C = O_Q + Q_LORA
O_KR = O_C + KV_LORA
O_KB = O_KR + LANE
O_VB = O_KB + KV_HEADS_B * HD_B
O_KI = O_VB + KV_HEADS_B * HD_B
O_WI = O_KI + LANE
P1_COLS = O_WI + LANE

NT_DIMS = (((1,), (1,)), ((), ()))


def _cparams(sem):
    return pltpu.CompilerParams(dimension_semantics=sem, vmem_limit_bytes=VMEM_LIMIT)


def _const_spec(shape):
    n = len(shape)
    return pl.BlockSpec(shape, lambda *_: (0,) * n)


def _rms(x, g):
    return x * lax.rsqrt(jnp.mean(x * x, axis=-1, keepdims=True) + RMS_EPS) * g


def _layer_norm(x, g, b):
    mu = jnp.mean(x, axis=-1, keepdims=True)
    d = x - mu
    var = jnp.mean(d * d, axis=-1, keepdims=True)
    return d * lax.rsqrt(var + LN_EPS) * g + b


def _rope(x, tab_ref, half):
    w = x.shape[-1]
    return (x * tab_ref[0] + pltpu.roll(x, w - half, 1) * tab_ref[1] + pltpu.roll(x, half, 1) * tab_ref[2])


def _rope_table(pos, width, head_dim, rot):
    half = rot // 2
    inv = jnp.power(ROPE_THETA, -jnp.arange(half, dtype=F32) / half)
    ang = pos.astype(F32)[:, None] * inv[None, :]
    cos, sin = jnp.cos(ang), jnp.sin(ang)
    t = pos.shape[0]
    rest = jnp.zeros((t, head_dim - rot), F32)
    zh = jnp.zeros((t, half), F32)
    c = jnp.concatenate([cos, cos, rest + 1.0], axis=1)
    s1 = jnp.concatenate([-sin, zh, rest], axis=1)
    s2 = jnp.concatenate([zh, sin, rest], axis=1)
    return jnp.stack([jnp.tile(a, (1, width // head_dim)) for a in (c, s1, s2)])


def _sort_key(s):
    bits = pltpu.bitcast(s + 0.0, I32)
    return jnp.where(bits < 0, bits ^ 0x7FFFFFFF, bits)


def _kth_largest_key(key, k):
    kf = float(k)

    def count_ge(c):
        return jnp.sum((key >= c).astype(F32), axis=1, keepdims=True)

    ans = jnp.where(count_ge(jnp.zeros_like(key[:, :1])) >= kf, 0, INT_MIN).astype(I32)

    def body(i, ans):
        cand = ans | jnp.left_shift(jnp.int32(1), 30 - i)
        return jnp.where(count_ge(cand) >= kf, cand, ans)

    return lax.fori_loop(0, 31, body, ans)


def _top_rows(vals, k):
    r = vals.shape[0]
    rid = lax.broadcasted_iota(I32, vals.shape, 0).astype(F32)
    tops, ids = [], []
    for _ in range(k):
        m = jnp.max(vals, axis=0, keepdims=True)
        idx = jnp.min(jnp.where(vals == m, rid, float(r)), axis=0, keepdims=True)
        vals = jnp.where(rid == idx, -jnp.inf, vals)
        tops.append(m)
        ids.append(idx)
    return jnp.concatenate(tops, axis=0), jnp.concatenate(ids, axis=0)


def _proj1_kernel(x_ref, w_ref, qg_ref, kvg_ref, ig_ref, ib_ref, tkr_ref, tkb_ref, tki_ref,
                  qn_ref, ckv_ref, kr_ref, kb_ref, vb_ref, ki_ref, wi_ref,
                  ckvb_ref, krb_ref, kbb_ref, vbb_ref, kib_ref):
    z = jnp.dot(x_ref[...].astype(BF16), w_ref[...], preferred_element_type=F32)
    qn_ref[...] = _rms(z[:, O_Q:O_Q + Q_LORA], qg_ref[...]).astype(BF16)
    ckv = _rms(z[:, O_C:O_C + KV_LORA], kvg_ref[...])
    ckv_ref[...] = ckv
    ckvb_ref[...] = ckv.astype(BF16)
    kr = _rope(z[:, O_KR:O_KR + LANE], tkr_ref, ROPE_DIM // 2)[:, :ROPE_DIM]
    kr_ref[...] = kr
    krb_ref[...] = kr.astype(BF16)
    kb = _rope(z[:, O_KB:O_KB + KV_HEADS_B * HD_B], tkb_ref, ROT_B // 2)
    kb_ref[...] = kb
    kbb_ref[...] = kb.astype(BF16)
    vb = z[:, O_VB:O_VB + KV_HEADS_B * HD_B]
    vb_ref[...] = vb
    vbb_ref[...] = vb.astype(BF16)
    zi = z[:, O_KI:O_KI + LANE]
    real = lax.broadcasted_iota(I32, zi.shape, 1) < IDX_DIM
    mu = jnp.sum(zi, axis=-1, keepdims=True) * (1.0 / IDX_DIM)
    d = jnp.where(real, zi - mu, 0.0)
    var = jnp.sum(d * d, axis=-1, keepdims=True) * (1.0 / IDX_DIM)
    ki = _rope(d * lax.rsqrt(var + LN_EPS) * ig_ref[...] + ib_ref[...], tki_ref, ROT_IDX // 2)[:, :IDX_DIM]
    ki_ref[...] = ki
    kib_ref[...] = ki.astype(BF16)
    wi_ref[...] = z[:, O_WI:O_WI + H_IDX] * W_IDX_SCALE


def _proj2_kernel(x_ref, w_ref, tqb_ref, tqi_ref, qb_ref, qi_ref):
    z = jnp.dot(x_ref[...].astype(BF16), w_ref[...], preferred_element_type=F32)
    nb = H_B * HD_B
    qb_ref[...] = _rope(z[:, :nb], tqb_ref, ROT_B // 2).astype(BF16)
    qi_ref[...] = (_rope(z[:, nb:], tqi_ref, ROT_IDX // 2) * IDX_SCALE).astype(BF16)


def _gate_kernel(x_ref, w_ref, g_ref):
    z = jnp.dot(x_ref[...].astype(BF16), w_ref[...], preferred_element_type=F32)
    g_ref[...] = jax.nn.sigmoid(z)


def _qpath_kernel(qn_ref, wuq_ref, wukt_ref, tqr_ref, ql_ref, qr_ref):
    qa = jnp.dot(qn_ref[...], wuq_ref[...], preferred_element_type=F32)
    n_nope = H_A * NOPE_DIM
    qr = _rope(qa[:, n_nope:], tqr_ref, ROPE_DIM // 2).astype(BF16)
    for h in range(H_A):
        qr_ref[0, h] = qr[:, h * ROPE_DIM:(h + 1) * ROPE_DIM]
        nope = qa[:, h * NOPE_DIM:(h + 1) * NOPE_DIM].astype(BF16)
        ql_ref[0, h] = jnp.dot(nope, wukt_ref[h], preferred_element_type=F32).astype(BF16)


def _tab_spec(tab, tm, tiles_per_batch):
    w = tab.shape[-1]
    if tab.shape[1] == 1:
        return pl.BlockSpec((3, 1, w), lambda i: (0, 0, 0))
    return pl.BlockSpec((3, tm, w), lambda i: (0, i % tiles_per_batch, 0))


def _project(x2, pos, n_batch, weights):
    n = x2.shape[0]
    t = pos.shape[0]
    tm = min(256, n if t == 1 else t)
    tpb = max(1, t // tm)
    w1, w2, wg, qg, kvg, ig, ib, wuq, wukt = weights
    tkr = _rope_table(pos, LANE, LANE, ROPE_DIM)
    tkb = _rope_table(pos, KV_HEADS_B * HD_B, HD_B, ROT_B)
    tki = _rope_table(pos, LANE, LANE, ROT_IDX)
    tqb = _rope_table(pos, H_B * HD_B, HD_B, ROT_B)
    tqi = _rope_table(pos, H_IDX * IDX_DIM, IDX_DIM, ROT_IDX)
    tqr = _rope_table(pos, H_A * ROPE_DIM, ROPE_DIM, ROPE_DIM)
    row = lambda w: pl.BlockSpec((tm, w), lambda i: (i, 0))
    xspec = row(D_MODEL)
    kvw = KV_HEADS_B * HD_B
    outs1 = pl.pallas_call(
        _proj1_kernel,
        grid=(n // tm,),
        in_specs=[xspec, _const_spec((D_MODEL, P1_COLS)), _const_spec((1, Q_LORA)), _const_spec((1, KV_LORA)),
                  _const_spec((1, LANE)), _const_spec((1, LANE)),
                  _tab_spec(tkr, tm, tpb), _tab_spec(tkb, tm, tpb), _tab_spec(tki, tm, tpb)],
        out_specs=[row(Q_LORA), row(KV_LORA), row(ROPE_DIM), row(kvw), row(kvw), row(IDX_DIM), row(H_IDX),
                   row(KV_LORA), row(ROPE_DIM), row(kvw), row(kvw), row(IDX_DIM)],
        out_shape=[jax.ShapeDtypeStruct((n, Q_LORA), BF16), jax.ShapeDtypeStruct((n, KV_LORA), F32),
                   jax.ShapeDtypeStruct((n, ROPE_DIM), F32), jax.ShapeDtypeStruct((n, kvw), F32),
                   jax.ShapeDtypeStruct((n, kvw), F32), jax.ShapeDtypeStruct((n, IDX_DIM), F32),
                   jax.ShapeDtypeStruct((n, H_IDX), F32),
                   jax.ShapeDtypeStruct((n, KV_LORA), BF16), jax.ShapeDtypeStruct((n, ROPE_DIM), BF16),
                   jax.ShapeDtypeStruct((n, kvw), BF16), jax.ShapeDtypeStruct((n, kvw), BF16),
                   jax.ShapeDtypeStruct((n, IDX_DIM), BF16)],
        compiler_params=_cparams(("parallel",)),
        name="proj1",
    )(x2, w1, qg, kvg, ig, ib, tkr, tkb, tki)
    qn, ckv, kr, kb, vb, ki, wi, ckv_b, kr_b, kb_b, vb_b, ki_b = outs1

    qb, qi = pl.pallas_call(
        _proj2_kernel,
        grid=(n // tm,),
        in_specs=[xspec, _const_spec(w2.shape), _tab_spec(tqb, tm, tpb), _tab_spec(tqi, tm, tpb)],
        out_specs=[row(H_B * HD_B), row(H_IDX * IDX_DIM)],
        out_shape=[jax.ShapeDtypeStruct((n, H_B * HD_B), BF16), jax.ShapeDtypeStruct((n, H_IDX * IDX_DIM), BF16)],
        compiler_params=_cparams(("parallel",)),
        name="proj2",
    )(x2, w2, tqb, tqi)

    tn = D_MODEL
    tg = min(512, n)
    gates = pl.pallas_call(
        _gate_kernel,
        grid=(2 * D_MODEL // tn, n // tg),
        in_specs=[pl.BlockSpec((tg, D_MODEL), lambda j, i: (i, 0)), pl.BlockSpec((D_MODEL, tn), lambda j, i: (0, j))],
        out_specs=pl.BlockSpec((tg, tn), lambda j, i: (i, j)),
        out_shape=jax.ShapeDtypeStruct((n, 2 * D_MODEL), F32),
        compiler_params=_cparams(("parallel", "parallel")),
        name="gates",
    )(x2, wg)

    t_eff = n // n_batch
    ql, qr = pl.pallas_call(
        _qpath_kernel,
        grid=(n // tm,),
        in_specs=[row(Q_LORA), _const_spec(wuq.shape), _const_spec(wukt.shape), _tab_spec(tqr, tm, tpb)],
        out_specs=[pl.BlockSpec((1, H_A, tm, KV_LORA), lambda i: (i // (t_eff // tm), 0, i % (t_eff // tm), 0)),
                   pl.BlockSpec((1, H_A, tm, ROPE_DIM), lambda i: (i // (t_eff // tm), 0, i % (t_eff // tm), 0))],
        out_shape=[jax.ShapeDtypeStruct((n_batch, H_A, t_eff, KV_LORA), BF16),
                   jax.ShapeDtypeStruct((n_batch, H_A, t_eff, ROPE_DIM), BF16)],
        compiler_params=_cparams(("parallel",)),
        name="qpath",
    )(qn, wuq, wukt, tqr)
    return dict(ckv=ckv, kr=kr, kb=kb, vb=vb, ki=ki, wi=wi, ckv_b=ckv_b, kr_b=kr_b, kb_b=kb_b, vb_b=vb_b,
                ki_b=ki_b, qb=qb, qi=qi, gates=gates, ql=ql, qr=qr)


def _mla_prompt_kernel(ql_ref, qr_ref, c_ref, kr_ref, o_ref, m_sc, l_sc, acc_sc, *, tq, tk):
    i, j = pl.program_id(1), pl.program_id(2)
    rows = H_A * tq

    @pl.when(j == 0)
    def _():
        m_sc[...] = jnp.full_like(m_sc, -jnp.inf)
        l_sc[...] = jnp.zeros_like(l_sc)
        acc_sc[...] = jnp.zeros_like(acc_sc)

    @pl.when(j * tk < (i + 1) * tq)
    def _():
        ql = ql_ref[0].reshape(rows, KV_LORA)
        qr = qr_ref[0].reshape(rows, ROPE_DIM)
        c = c_ref[0]
        s = (lax.dot_general(ql, c, NT_DIMS, preferred_element_type=F32)
             + lax.dot_general(qr, kr_ref[0], NT_DIMS, preferred_element_type=F32)) * MLA_SCALE
        q_pos = i * tq + lax.broadcasted_iota(I32, (H_A, tq, tk), 1).reshape(rows, tk)
        k_pos = j * tk + lax.broadcasted_iota(I32, (rows, tk), 1)
        s = jnp.where(k_pos <= q_pos, s, -jnp.inf)
        m_new = jnp.maximum(m_sc[...], jnp.max(s, axis=1, keepdims=True))
        a = jnp.exp(m_sc[...] - m_new)
        p = jnp.exp(s - m_new)
        l_sc[...] = a * l_sc[...] + jnp.sum(p, axis=1, keepdims=True)
        acc_sc[...] = a * acc_sc[...] + jnp.dot(p.astype(BF16), c, preferred_element_type=F32)
        m_sc[...] = m_new

    @pl.when(j == pl.num_programs(2) - 1)
    def _():
        o_ref[0] = (acc_sc[...] / l_sc[...]).astype(BF16).reshape(H_A, tq, KV_LORA)


def _mla_prompt(ql, qr, ckv_b, kr_b):
    b, _, t, _ = ql.shape
    tq = min(256, t)
    tk = min(512, t)
    last = lambda i, j: jnp.minimum(j, ((i + 1) * tq - 1) // tk)
    return pl.pallas_call(
        functools.partial(_mla_prompt_kernel, tq=tq, tk=tk),
        grid=(b, t // tq, t // tk),
        in_specs=[pl.BlockSpec((1, H_A, tq, KV_LORA), lambda bb, i, j: (bb, 0, i, 0)),
                  pl.BlockSpec((1, H_A, tq, ROPE_DIM), lambda bb, i, j: (bb, 0, i, 0)),
                  pl.BlockSpec((1, tk, KV_LORA), lambda bb, i, j: (bb, last(i, j), 0)),
                  pl.BlockSpec((1, tk, ROPE_DIM), lambda bb, i, j: (bb, last(i, j), 0))],
        out_specs=pl.BlockSpec((1, H_A, tq, KV_LORA), lambda bb, i, j: (bb, 0, i, 0)),
        out_shape=jax.ShapeDtypeStruct((b, H_A, t, KV_LORA), BF16),
        scratch_shapes=[pltpu.VMEM((H_A * tq, 1), F32), pltpu.VMEM((H_A * tq, 1), F32),
                        pltpu.VMEM((H_A * tq, KV_LORA), F32)],
        compiler_params=_cparams(("parallel", "parallel", "arbitrary")),
        name="mla_prompt",
    )(ql, qr, ckv_b, kr_b)


DSA_KEY_SPAN = 512


def _dsa_prompt_kernel(qi_ref, wi_ref, qb_ref, ki_ref, kb_ref, vb_ref, o_ref, *, tq, k_sel):
    i = pl.program_id(1)
    t_full = ki_ref.shape[1]
    span = min(DSA_KEY_SPAN, t_full)
    tiles_per_span = span // tq
    for v in range(t_full // span):
        pl.when(i // tiles_per_span == v)(
            functools.partial(_dsa_prompt_body, qi_ref, wi_ref, qb_ref, ki_ref, kb_ref, vb_ref, o_ref,
                              tq=tq, k_sel=k_sel, t_all=(v + 1) * span))


def _dsa_prompt_body(qi_ref, wi_ref, qb_ref, ki_ref, kb_ref, vb_ref, o_ref, *, tq, k_sel, t_all):
    i = pl.program_id(1)
    ki = ki_ref[0, :t_all, :]
    wi = wi_ref[...]
    score = jnp.zeros((tq, t_all), F32)
    for h in range(H_IDX):
        d = lax.dot_general(qi_ref[:, h * IDX_DIM:(h + 1) * IDX_DIM], ki, NT_DIMS, preferred_element_type=F32)
        score = score + wi[:, h:h + 1] * jnp.maximum(d, 0.0)
    q_pos = i * tq + lax.broadcasted_iota(I32, (tq, t_all), 0)
    k_pos = lax.broadcasted_iota(I32, (tq, t_all), 1)
    causal = k_pos <= q_pos
    key = jnp.where(causal, _sort_key(score), INT_MIN)
    thr = _kth_largest_key(key, k_sel)
    keep = (causal & (key >= thr))[None]
    hpg = H_B // KV_HEADS_B
    for g in range(KV_HEADS_B):
        qg = jnp.concatenate([qb_ref[:, (g * hpg + jj) * HD_B:(g * hpg + jj + 1) * HD_B] for jj in range(hpg)], axis=0)
        kg = kb_ref[0, :t_all, g * HD_B:(g + 1) * HD_B]
        vg = vb_ref[0, :t_all, g * HD_B:(g + 1) * HD_B]
        s = lax.dot_general(qg, kg, NT_DIMS, preferred_element_type=F32) * DSA_SCALE
        s = jnp.where(keep, s.reshape(hpg, tq, t_all), -jnp.inf).reshape(hpg * tq, t_all)
        p = jnp.exp(s - jnp.max(s, axis=1, keepdims=True))
        l = jnp.sum(p, axis=1, keepdims=True)
        o = (jnp.dot(p.astype(BF16), vg, preferred_element_type=F32) / l).astype(BF16)
        for jj in range(hpg):
            o_ref[:, (g * hpg + jj) * HD_B:(g * hpg + jj + 1) * HD_B] = o[jj * tq:(jj + 1) * tq]


def _dsa_prompt(qi, wi, qb, ki_b, kb_b, vb_b, b, t):
    tq = min(128, t)
    k_sel = min(TOPK_MAX, t // 4)
    tpb = t // tq
    row = lambda w: pl.BlockSpec((tq, w), lambda bb, i: (bb * tpb + i, 0))
    per_b = lambda w: pl.BlockSpec((1, t, w), lambda bb, i: (bb, 0, 0))
    kvw = KV_HEADS_B * HD_B
    return pl.pallas_call(
        functools.partial(_dsa_prompt_kernel, tq=tq, k_sel=k_sel),
        grid=(b, tpb),
        in_specs=[row(H_IDX * IDX_DIM), row(H_IDX), row(H_B * HD_B), per_b(IDX_DIM), per_b(kvw), per_b(kvw)],
        out_specs=row(H_B * HD_B),
        out_shape=jax.ShapeDtypeStruct((b * t, H_B * HD_B), BF16),
        compiler_params=_cparams(("parallel", "arbitrary")),
        name="dsa_prompt",
    )(qi, wi, qb, ki_b.reshape(b, t, IDX_DIM), kb_b.reshape(b, t, kvw), vb_b.reshape(b, t, kvw))


def _merge_kernel(ol_ref, ob_ref, g_ref, x_ref, wuv_ref, wao_ref, wbo_ref, wout_ref, lg_ref, lb_ref, h_ref):
    ta = jnp.concatenate([jnp.dot(ol_ref[0, h], wuv_ref[h], preferred_element_type=F32) for h in range(H_A)],
                         axis=1).astype(BF16)
    ya = jnp.dot(ta, wao_ref[...], preferred_element_type=F32)
    yb = jnp.dot(ob_ref[...], wbo_ref[...], preferred_element_type=F32)
    m = (g_ref[:, :D_MODEL] * ya + g_ref[:, D_MODEL:] * yb).astype(BF16)
    mix = jnp.dot(m, wout_ref[...], preferred_element_type=F32)
    h_ref[...] = _layer_norm(ALPHA * x_ref[...] + mix, lg_ref[...], lb_ref[...])


def _merge(o_lat, o_b, gates, x2, weights):
    b, _, t, _ = o_lat.shape
    tm = min(256, t)
    tpb = t // tm
    wuv, wao, wbo, wout, lg, lb = weights
    row = lambda w: pl.BlockSpec((tm, w), lambda i: (i, 0))
    once = lambda a: pl.BlockSpec(a.shape, lambda i: (0,) * a.ndim, pipeline_mode=pl.Buffered(1))
    return pl.pallas_call(
        _merge_kernel,
        grid=(b * tpb,),
        in_specs=[pl.BlockSpec((1, H_A, tm, KV_LORA), lambda i: (i // tpb, 0, i % tpb, 0)),
                  row(H_B * HD_B), row(2 * D_MODEL), row(D_MODEL),
                  once(wuv), once(wao), once(wbo), once(wout), once(lg), once(lb)],
        out_specs=row(D_MODEL),
        out_shape=jax.ShapeDtypeStruct((b * t, D_MODEL), F32),
        compiler_params=_cparams(("parallel",)),
        name="merge",
    )(o_lat, o_b, gates, x2, wuv, wao, wbo, wout, lg, lb)


def _peer_route_kernel(h_ref, wq_ref, sk_ref, e_ref, g_ref):
    q = jnp.dot(h_ref[...].astype(BF16), wq_ref[...], preferred_element_type=F32).astype(BF16)
    half = PEER_DQ // 2
    gates, experts = [], []
    for h in range(PEER_HEADS):
        sv, si = [], []
        for c in range(2):
            qhc = q[:, (2 * h + c) * half:(2 * h + c + 1) * half]
            s_t = lax.dot_general(sk_ref[h, c], qhc, NT_DIMS, preferred_element_type=F32)
            v, ix = _top_rows(s_t, PEER_TOPK)
            sv.append(v)
            si.append(ix)
        kk = PEER_TOPK
        tm = sv[0].shape[1]
        a8 = lax.broadcasted_iota(I32, (8, tm), 0)
        a16 = lax.broadcasted_iota(I32, (kk, tm), 0)
        cand = [sv[0] + sv[1][0:1]]
        cidx = [si[0] * float(N_KEYS) + si[1][0:1]]
        flat = [a16 * kk]
        for b in range(1, 8):
            ok = (a8 + 1) * (b + 1) <= kk
            cand.append(jnp.where(ok, sv[0][0:8] + sv[1][b:b + 1], -jnp.inf))
            cidx.append(si[0][0:8] * float(N_KEYS) + si[1][b:b + 1])
            flat.append(a8 * kk + b)
        cand.append(sv[0][0:1] + sv[1][8:kk])
        cidx.append(si[0][0:1] * float(N_KEYS) + si[1][8:kk])
        flat.append(a8 + 8)
        cand = jnp.concatenate(cand, axis=0)
        cidx = jnp.concatenate(cidx, axis=0)
        rid = jnp.concatenate(flat, axis=0).astype(F32)
        fv, fe = [], []
        for _ in range(PEER_TOPK):
            m = jnp.max(cand, axis=0, keepdims=True)
            pos = jnp.min(jnp.where(cand == m, rid, float(kk * kk)), axis=0, keepdims=True)
            hit = rid == pos
            fe.append(jnp.sum(jnp.where(hit, cidx, 0.0), axis=0, keepdims=True))
            cand = jnp.where(hit, -jnp.inf, cand)
            fv.append(m)
        fv = jnp.concatenate(fv, axis=0)
        p = jnp.exp(fv - fv[0:1])
        gates.append(p / jnp.sum(p, axis=0, keepdims=True))
        experts.append(jnp.concatenate(fe, axis=0))
    g_ref[...] = jnp.concatenate(gates, axis=0).T
    e_ref[...] = jnp.concatenate(experts, axis=0).T.astype(I32)


def _peer_route(h2, wq, sk):
    n = h2.shape[0]
    tm = min(128, n)
    once = lambda a: pl.BlockSpec(a.shape, lambda i: (0,) * a.ndim)
    return pl.pallas_call(
        _peer_route_kernel,
        grid=(n // tm,),
        in_specs=[pl.BlockSpec((tm, D_MODEL), lambda i: (i, 0)), once(wq), once(sk)],
        out_specs=[pl.BlockSpec((tm, N_SEL), lambda i: (i, 0)), pl.BlockSpec((tm, N_SEL), lambda i: (i, 0))],
        out_shape=[jax.ShapeDtypeStruct((n, N_SEL), I32), jax.ShapeDtypeStruct((n, N_SEL), F32)],
        compiler_params=_cparams(("parallel",)),
        name="peer_route",
    )(h2, wq, sk)


N_SEL = PEER_HEADS * PEER_TOPK
PEER_TT = 32
PEER_SLOTS = 4
PEER_AHEAD = PEER_SLOTS - 1
IDX_RING = 3
PEER_CHUNK = 16
HALF_D = D_MODEL // 2


def _pack_bf16_pairs(a):
    h = a.shape[1] // 2
    bits = lambda x: lax.bitcast_convert_type(x.astype(BF16), jnp.uint16).astype(jnp.uint32)
    return lax.bitcast_convert_type(bits(a[:, :h]) | (bits(a[:, h:]) << 16), I32)


def _low_bf16(w):
    return pltpu.bitcast(w << 16, F32)


def _high_bf16(w):
    return pltpu.bitcast(w & -65536, F32)


def _peer_expert_kernel(e_hbm, h_ref, g_ref, tab_hbm, lg_ref, lb_ref, y_ref,
                        idx_sm, *rest, n_total, n_steps):
    buf, (po_sc, sem_i, sem_g) = rest[:PEER_SLOTS], rest[PEER_SLOTS:]
    i = pl.program_id(0)
    tt = PEER_TT

    def idx_copy(step):
        ring = step % IDX_RING
        return pltpu.make_async_copy(e_hbm.at[step], idx_sm.at[pl.ds(ring * tt, tt)], sem_i.at[ring])

    def row_copy(e, slot, k):
        return pltpu.make_async_copy(tab_hbm.at[e], buf[slot].at[pl.ds(k, 1)], sem_g.at[slot])

    def idx_row(n):
        n = jnp.minimum(n, n_total - 1)
        return ((n // tt) % IDX_RING) * tt + n % tt

    def issue(row, slot, k0, k1):
        for k in range(k0, k1):
            row_copy(idx_sm[row, k], slot, k).start(priority=k % 2)

    def wait(slot):
        other = buf[(slot + 1) % PEER_SLOTS]
        pltpu.make_async_copy(other, buf[slot], sem_g.at[slot]).wait()

    @pl.when(i == 0)
    def _():
        idx_copy(0).start()
        if n_steps > 1:
            idx_copy(1).start()
        idx_copy(0).wait()
        for t in range(PEER_AHEAD):
            issue(idx_row(t), t, 0, N_SEL)

    @pl.when(i + 1 < n_steps)
    def _():
        idx_copy(i + 1).wait()

    @pl.when(i + 2 < n_steps)
    def _():
        idx_copy(i + 2).start()

    n_chunks = N_SEL // PEER_CHUNK
    per_chunk = N_SEL // (2 * n_chunks)

    def token(t, slot):
        wait(slot)
        row = idx_row(i * tt + t + PEER_AHEAD)
        ahead_slot = (slot + PEER_AHEAD) % PEER_SLOTS
        x = h_ref[pl.ds(t, 1), :]
        x_lo, x_hi = x[:, :HALF_D], x[:, HALF_D:]
        words = buf[slot]
        hk = []
        for c in range(n_chunks):
            wu = words[c * PEER_CHUNK:(c + 1) * PEER_CHUNK, :HALF_D]
            hk.append(jnp.sum(_low_bf16(wu) * x_lo + _high_bf16(wu) * x_hi, axis=1, keepdims=True))
            issue(row, ahead_slot, c * per_chunk, (c + 1) * per_chunk)
        hk = jnp.concatenate(hk, axis=0)
        gcol = jnp.transpose(jnp.broadcast_to(g_ref[pl.ds(t, 1), :], (N_SEL, N_SEL)))[:, 0:1]
        a = gcol * (0.5 * hk * (1.0 + lax.erf(hk * math.sqrt(0.5))))
        acc_lo = jnp.zeros((8, HALF_D), F32)
        acc_hi = jnp.zeros((8, HALF_D), F32)
        for c in range(n_chunks):
            wv = words[c * PEER_CHUNK:(c + 1) * PEER_CHUNK, HALF_D:]
            ac = a[c * PEER_CHUNK:(c + 1) * PEER_CHUNK]
            for r in range(0, PEER_CHUNK, 8):
                acc_lo = acc_lo + _low_bf16(wv[r:r + 8]) * ac[r:r + 8]
                acc_hi = acc_hi + _high_bf16(wv[r:r + 8]) * ac[r:r + 8]
            issue(row, ahead_slot, (n_chunks + c) * per_chunk, (n_chunks + c + 1) * per_chunk)
        po_sc[pl.ds(t, 1), :] = jnp.concatenate(
            [jnp.sum(acc_lo, axis=0, keepdims=True), jnp.sum(acc_hi, axis=0, keepdims=True)], axis=1)

    def group(g, carry):
        for j in range(PEER_SLOTS):
            token(g * PEER_SLOTS + j, j)
        return carry

    lax.fori_loop(0, tt // PEER_SLOTS, group, 0)

    @pl.when(i == n_steps - 1)
    def _():
        for j in range(PEER_AHEAD):
            wait(j)

    y_ref[...] = _layer_norm(ALPHA * h_ref[...] + po_sc[...], lg_ref[...], lb_ref[...])


def _peer_experts(eidx, gate, h2, table, lg, lb):
    n = h2.shape[0]
    tt = PEER_TT
    assert n % tt == 0 and tt % PEER_SLOTS == 0
    steps = n // tt
    row = lambda w: pl.BlockSpec((tt, w), lambda i: (i, 0))
    anyspec = pl.BlockSpec(memory_space=pl.ANY)
    return pl.pallas_call(
        functools.partial(_peer_expert_kernel, n_total=n, n_steps=steps),
        grid=(steps,),
        in_specs=[anyspec, row(D_MODEL), row(N_SEL), anyspec,
                  _const_spec((1, D_MODEL)), _const_spec((1, D_MODEL))],
        out_specs=row(D_MODEL),
        out_shape=jax.ShapeDtypeStruct((n, D_MODEL), F32),
        scratch_shapes=[pltpu.SMEM((IDX_RING * tt, N_SEL), I32),
                        *[pltpu.VMEM((N_SEL, D_MODEL), I32) for _ in range(PEER_SLOTS)],
                        pltpu.VMEM((tt, D_MODEL), F32),
                        pltpu.SemaphoreType.DMA((IDX_RING,)),
                        pltpu.SemaphoreType.DMA((PEER_SLOTS,))],
        compiler_params=_cparams(("arbitrary",)),
        name="peer_experts",
    )(eidx.reshape(steps, tt, N_SEL), h2, gate, table, lg, lb)


def _peer_layer(h2, peer_w):
    wq, sk, table, lg, lb = peer_w
    eidx, gate = _peer_route(h2, wq, sk)
    return _peer_experts(eidx, gate, h2, table, lg, lb)


PAGES_PER_STEP = 16


def _stream_pages(pt_ref, pools, bufs, sem, page_window):
    pp = PAGES_PER_STEP
    nps = pl.num_programs(1)
    step = pl.program_id(0) * nps + pl.program_id(1)
    total = pl.num_programs(0) * nps
    slot = step % 2

    def start(s, into):
        seq, group = s // nps, s % nps
        for j in range(pp):
            page = pt_ref[seq, group * pp + j]
            for a, pool in enumerate(pools):
                pltpu.make_async_copy(pool.at[page], page_window(a, bufs[a].at[into], j),
                                      sem.at[a, into]).start(priority=j % 2)

    @pl.when(step == 0)
    def _():
        start(step, slot)

    @pl.when(step + 1 < total)
    def _():
        start(step + 1, 1 - slot)

    for a in range(len(pools)):
        pltpu.make_async_copy(bufs[a].at[1 - slot], bufs[a].at[slot], sem.at[a, slot]).wait()
    return slot


def _rows_window(rows_per_page):
    return lambda buf, j: buf.at[pl.ds(j * rows_per_page, rows_per_page)]


def _lanes_window(buf, j):
    return buf.at[:, pl.ds(j * PAGE_SIZE, PAGE_SIZE)]


def _sample_pass1_kernel(pt_ref, ql_ref, qr_ref, qi_ref, wi_ref, cn_ref, krn_ref, c_hbm, kr_hbm, ki_hbm,
                         o_ref, sc_ref, m_sc, l_sc, acc_sc, cbuf, krbuf, kibuf, sem):
    p_id = pl.program_id(1)
    windows = (_rows_window(PAGE_SIZE), _lanes_window, _lanes_window)
    slot = _stream_pages(pt_ref, (c_hbm, kr_hbm, ki_hbm), (cbuf, krbuf, kibuf), sem,
                         lambda a, buf, j: windows[a](buf, j))

    @pl.when(p_id == 0)
    def _():
        m_sc[...] = jnp.full_like(m_sc, -jnp.inf)
        l_sc[...] = jnp.zeros_like(l_sc)
        acc_sc[...] = jnp.zeros_like(acc_sc)

    ql, qr = ql_ref[0], qr_ref[0]

    def absorb(s, weighted_values):
        s = s * MLA_SCALE
        m_new = jnp.maximum(m_sc[...], jnp.max(s, axis=1, keepdims=True))
        a = jnp.exp(m_sc[...] - m_new)
        p = jnp.exp(s - m_new)
        l_sc[...] = a * l_sc[...] + jnp.sum(p, axis=1, keepdims=True)
        acc_sc[...] = a * acc_sc[...] + weighted_values(p.astype(BF16))
        m_sc[...] = m_new

    c = cbuf[slot].astype(BF16)
    kr_t = krbuf[slot].astype(BF16)
    absorb(lax.dot_general(ql, c, NT_DIMS, preferred_element_type=F32)
           + jnp.dot(qr, kr_t, preferred_element_type=F32),
           lambda p: jnp.dot(p, c, preferred_element_type=F32))
    ki_t = kibuf[slot].astype(BF16)
    d = jnp.dot(qi_ref[0], ki_t, preferred_element_type=F32)
    sc_ref[0] = jnp.sum(wi_ref[0] * jnp.maximum(d, 0.0), axis=0, keepdims=True)

    @pl.when(p_id == pl.num_programs(1) - 1)
    def _():
        cn = cn_ref[0].astype(F32)
        s_new = (jnp.sum(ql.astype(F32) * cn, axis=1, keepdims=True)
                 + jnp.sum(qr.astype(F32) * krn_ref[0].astype(F32), axis=1, keepdims=True))
        absorb(s_new, lambda p: p.astype(F32) * cn)
        o_ref[0] = (acc_sc[...] / l_sc[...]).astype(BF16)


def _sample_pass1(page_table, ql, qr, qi3, wi3, ckv_new, kr_new, c_pool, kr_pool, ki_pool):
    db, n_pages = page_table.shape
    pp = PAGES_PER_STEP
    per_b = lambda a: pl.BlockSpec((1,) + a.shape[1:], lambda b, p, pt: (b,) + (0,) * (a.ndim - 1))
    anyspec = pl.BlockSpec(memory_space=pl.ANY)
    span = pp * PAGE_SIZE
    grid_spec = pltpu.PrefetchScalarGridSpec(
        num_scalar_prefetch=1,
        grid=(db, n_pages // pp),
        in_specs=[per_b(ql), per_b(qr), per_b(qi3), per_b(wi3), per_b(ckv_new), per_b(kr_new),
                  anyspec, anyspec, anyspec],
        out_specs=[pl.BlockSpec((1, H_A, KV_LORA), lambda b, p, pt: (b, 0, 0)),
                   pl.BlockSpec((1, 1, span), lambda b, p, pt: (b, 0, p))],
        scratch_shapes=[pltpu.VMEM((H_A, 1), F32), pltpu.VMEM((H_A, 1), F32), pltpu.VMEM((H_A, KV_LORA), F32),
                        pltpu.VMEM((2, span, KV_LORA), F32), pltpu.VMEM((2, ROPE_DIM, span), F32),
                        pltpu.VMEM((2, IDX_DIM, span), F32), pltpu.SemaphoreType.DMA((3, 2))],
    )
    return pl.pallas_call(
        _sample_pass1_kernel,
        grid_spec=grid_spec,
        out_shape=[jax.ShapeDtypeStruct((db, H_A, KV_LORA), BF16),
                   jax.ShapeDtypeStruct((db, 1, n_pages * PAGE_SIZE), F32)],
        compiler_params=_cparams(("arbitrary", "arbitrary")),
        name="sample_pass1",
    )(page_table, ql, qr, qi3, wi3, ckv_new, kr_new, c_pool, kr_pool, ki_pool)


def _sample_select_kernel(sc_ref, qi_ref, wi_ref, kin_ref, thr_ref, new_ref, *, k_sel):
    past = sc_ref[...]
    qi = qi_ref[...].astype(F32)
    kn = kin_ref[...].astype(F32)
    d = jnp.sum(qi * kn, axis=2)
    s_new = jnp.sum(wi_ref[...] * jnp.maximum(d, 0.0), axis=1, keepdims=True)
    key_new = _sort_key(s_new)
    lane0 = lax.broadcasted_iota(I32, (past.shape[0], LANE), 1) == 0
    key = jnp.concatenate([_sort_key(past), jnp.where(lane0, key_new, INT_MIN)], axis=1)
    thr = _kth_largest_key(key, k_sel)
    thr_ref[...] = thr
    new_ref[...] = (key_new >= thr).astype(I32)


def _sample_select(scores, qi3, wi2, ki_new3, k_sel):
    db, past = scores.shape
    rows = min(8, db)
    return pl.pallas_call(
        functools.partial(_sample_select_kernel, k_sel=k_sel),
        grid=(db // rows,),
        in_specs=[pl.BlockSpec((rows, past), lambda i: (i, 0)),
                  pl.BlockSpec((rows, H_IDX, IDX_DIM), lambda i: (i, 0, 0)),
                  pl.BlockSpec((rows, H_IDX), lambda i: (i, 0)),
                  pl.BlockSpec((rows, 1, IDX_DIM), lambda i: (i, 0, 0))],
        out_specs=[pl.BlockSpec((rows, 1), lambda i: (i, 0)), pl.BlockSpec((rows, 1), lambda i: (i, 0))],
        out_shape=[jax.ShapeDtypeStruct((db, 1), I32), jax.ShapeDtypeStruct((db, 1), I32)],
        compiler_params=_cparams(("parallel",)),
        name="sample_select",
    )(scores, qi3, wi2, ki_new3)


def _sample_pass2_kernel(pt_ref, thr_ref, new_ref, qb_ref, sc_ref, kn_ref, vn_ref, k_hbm, v_hbm,
                         o_ref, m_sc, l_sc, acc_sc, kbuf, vbuf, sem):
    pp = PAGES_PER_STEP
    b, p_id = pl.program_id(0), pl.program_id(1)
    hpg = H_B // KV_HEADS_B
    window = _rows_window(PAGE_SIZE * KV_HEADS_B)
    slot = _stream_pages(pt_ref, (k_hbm, v_hbm), (kbuf, vbuf), sem, lambda a, buf, j: window(buf, j))

    @pl.when(p_id == 0)
    def _():
        m_sc[...] = jnp.full_like(m_sc, -jnp.inf)
        l_sc[...] = jnp.zeros_like(l_sc)
        acc_sc[...] = jnp.zeros_like(acc_sc)

    def absorb(g, s, keep, weighted_values):
        rows = slice(g * hpg, (g + 1) * hpg)
        s = jnp.where(keep, s * DSA_SCALE, -jnp.inf)
        m_old = m_sc[rows]
        m_new = jnp.maximum(m_old, jnp.max(s, axis=1, keepdims=True))
        m_safe = jnp.where(m_new == -jnp.inf, 0.0, m_new)
        a = jnp.exp(m_old - m_safe)
        p = jnp.exp(s - m_safe)
        l_sc[rows] = a * l_sc[rows] + jnp.sum(p, axis=1, keepdims=True)
        acc_sc[rows] = a * acc_sc[rows] + weighted_values(p.astype(BF16))
        m_sc[rows] = m_new

    keep = _sort_key(sc_ref[0]) >= thr_ref[b]
    for g in range(KV_HEADS_B):
        own = pl.ds(g, pp * PAGE_SIZE, stride=KV_HEADS_B)
        k = kbuf[slot, own, :].astype(BF16)
        v = vbuf[slot, own, :].astype(BF16)
        absorb(g, lax.dot_general(qb_ref[0, g * hpg:(g + 1) * hpg], k, NT_DIMS, preferred_element_type=F32),
               keep, lambda p, v=v: jnp.dot(p, v, preferred_element_type=F32))

    @pl.when(p_id == pl.num_programs(1) - 1)
    def _():
        for g in range(KV_HEADS_B):
            cols = slice(g * HD_B, (g + 1) * HD_B)
            kn = kn_ref[0, :, cols].astype(F32)
            vn = vn_ref[0, :, cols].astype(F32)
            s_new = jnp.sum(qb_ref[0, g * hpg:(g + 1) * hpg].astype(F32) * kn, axis=1, keepdims=True)
            absorb(g, s_new, new_ref[b] > 0, lambda p, vn=vn: p.astype(F32) * vn)
        o_ref[0] = (acc_sc[...] / l_sc[...]).astype(BF16)


def _sample_pass2(page_table, thr, sel_new, qb3, scores3, kb_new, vb_new, k_pool, v_pool):
    db, n_pages = page_table.shape
    pp = PAGES_PER_STEP
    kvw = KV_HEADS_B * HD_B
    per_b = lambda a: pl.BlockSpec((1,) + a.shape[1:], lambda b, p, *_: (b,) + (0,) * (a.ndim - 1))
    anyspec = pl.BlockSpec(memory_space=pl.ANY)
    group_rows = pp * PAGE_SIZE * KV_HEADS_B
    grid_spec = pltpu.PrefetchScalarGridSpec(
        num_scalar_prefetch=3,
        grid=(db, n_pages // pp),
        in_specs=[per_b(qb3), pl.BlockSpec((1, 1, pp * PAGE_SIZE), lambda b, p, *_: (b, 0, p)),
                  per_b(kb_new), per_b(vb_new), anyspec, anyspec],
        out_specs=pl.BlockSpec((1, H_B, HD_B), lambda b, p, *_: (b, 0, 0)),
        scratch_shapes=[pltpu.VMEM((H_B, 1), F32), pltpu.VMEM((H_B, 1), F32), pltpu.VMEM((H_B, HD_B), F32),
                        pltpu.VMEM((2, group_rows, HD_B), F32), pltpu.VMEM((2, group_rows, HD_B), F32),
                        pltpu.SemaphoreType.DMA((2, 2))],
    )
    return pl.pallas_call(
        _sample_pass2_kernel,
        grid_spec=grid_spec,
        out_shape=jax.ShapeDtypeStruct((db, H_B, HD_B), BF16),
        compiler_params=_cparams(("arbitrary", "arbitrary")),
        name="sample_pass2",
    )(page_table, thr, sel_new, qb3, scores3, kb_new, vb_new, k_pool, v_pool)


def kernel(x_prompt, x_sample, cache_kv_latent, cache_k_rope, cache_k, cache_v, cache_idx_k, page_table, w_in, mla_q_norm, mla_w_uq, mla_kv_norm, mla_w_uk, mla_w_uv, mla_w_o, idx_k_norm_g, idx_k_norm_b, dsa_w_o, w_out, ln1_g, ln1_b, peer_w_q, peer_sub_keys, peer_u, peer_v, ln2_g, ln2_b):
    b, t, _ = x_prompt.shape
    db, ts, _ = x_sample.shape
    assert ts == 1, "the sample group decodes one token per sequence"
    n_pages = page_table.shape[1]
    past_len = n_pages * PAGE_SIZE
    l = 0

    wq_, wkv_, wbq_, wbk_, wbv_, wiq_, wik_, wiw_, wg_ = jnp.split(w_in[l], IN_OFFSETS, axis=1)
    padc = lambda a: jnp.pad(a, ((0, 0), (0, LANE - a.shape[1])))
    w1 = jnp.concatenate([wq_, wkv_[:, :KV_LORA], padc(wkv_[:, KV_LORA:]), wbk_, wbv_, padc(wik_), padc(wiw_)],
                         axis=1).astype(BF16)
    w2 = jnp.concatenate([wbq_, wiq_], axis=1).astype(BF16)
    wg = wg_.astype(BF16)
    wuq = jnp.concatenate([mla_w_uq[l][:, :, :NOPE_DIM].reshape(Q_LORA, H_A * NOPE_DIM),
                           mla_w_uq[l][:, :, NOPE_DIM:].reshape(Q_LORA, H_A * ROPE_DIM)], axis=1).astype(BF16)
    wukt = jnp.transpose(mla_w_uk[l], (1, 2, 0)).astype(BF16)
    row1 = lambda a: a.reshape(1, -1).astype(F32)
    pad1 = lambda a: jnp.pad(a.astype(F32), (0, LANE - a.shape[0])).reshape(1, LANE)
    proj_w = (w1, w2, wg, row1(mla_q_norm[l]), row1(mla_kv_norm[l]), pad1(idx_k_norm_g[l]), pad1(idx_k_norm_b[l]),
              wuq, wukt)
    merge_w = (jnp.transpose(mla_w_uv[l], (1, 0, 2)).astype(BF16), mla_w_o[l].astype(BF16),
               dsa_w_o[l].astype(BF16), w_out[l].astype(BF16), row1(ln1_g[l]), row1(ln1_b[l]))
    peer_table = jnp.concatenate([_pack_bf16_pairs(peer_u[l]), _pack_bf16_pairs(peer_v[l])], axis=1)
    peer_table = peer_table.reshape(peer_table.shape[0], 1, D_MODEL)
    peer_w = (peer_w_q[l].astype(BF16), peer_sub_keys[l].astype(BF16), peer_table, row1(ln2_g[l]), row1(ln2_b[l]))
    kvw = KV_HEADS_B * HD_B

    xs2 = x_sample.reshape(db, D_MODEL)
    sp = _project(xs2, jnp.full((1,), past_len, I32), 1, proj_w)
    ql_s = jnp.transpose(sp["ql"][0], (1, 0, 2))
    qr_s = jnp.transpose(sp["qr"][0], (1, 0, 2))
    qi3 = sp["qi"].reshape(db, H_IDX, IDX_DIM)
    o_lat_s, scores = _sample_pass1(
        page_table, ql_s, qr_s, qi3, sp["wi"].reshape(db, H_IDX, 1),
        sp["ckv_b"].reshape(db, 1, KV_LORA), sp["kr_b"].reshape(db, 1, ROPE_DIM),
        cache_kv_latent[l], jnp.swapaxes(cache_k_rope[l], 1, 2), jnp.swapaxes(cache_idx_k[l], 1, 2))
    k_sel_s = min(TOPK_MAX, (past_len + 1) // 4)
    thr, sel_new = _sample_select(scores.reshape(db, past_len), qi3, sp["wi"], sp["ki_b"].reshape(db, 1, IDX_DIM),
                                  k_sel_s)
    o_b_s = _sample_pass2(page_table, thr.reshape(db), sel_new.reshape(db), sp["qb"].reshape(db, H_B, HD_B), scores,
                          sp["kb_b"].reshape(db, 1, kvw), sp["vb_b"].reshape(db, 1, kvw),
                          cache_k[l].reshape(-1, PAGE_SIZE * KV_HEADS_B, HD_B),
                          cache_v[l].reshape(-1, PAGE_SIZE * KV_HEADS_B, HD_B))
    o_lat_s4 = jnp.transpose(o_lat_s, (1, 0, 2)).reshape(1, H_A, db, KV_LORA)
    h_s = _merge(o_lat_s4, o_b_s.reshape(db, H_B * HD_B), sp["gates"], xs2, merge_w)
    y_s = _peer_layer(h_s, peer_w).reshape(db, 1, D_MODEL)

    xp2 = x_prompt.reshape(b * t, D_MODEL)
    pp_ = _project(xp2, jnp.arange(t, dtype=I32), b, proj_w)
    o_lat_p = _mla_prompt(pp_["ql"], pp_["qr"], pp_["ckv_b"].reshape(b, t, KV_LORA), pp_["kr_b"].reshape(b, t, ROPE_DIM))
    o_b_p = _dsa_prompt(pp_["qi"], pp_["wi"], pp_["qb"], pp_["ki_b"], pp_["kb_b"], pp_["vb_b"], b, t)
    h_p = _merge(o_lat_p, o_b_p, pp_["gates"], xp2, merge_w)
    y_p = _peer_layer(h_p, peer_w).reshape(b, t, D_MODEL)

    def rows(d, bb, tt_):
        return (d["ckv"].reshape(1, bb, tt_, KV_LORA), d["kr"].reshape(1, bb, tt_, ROPE_DIM),
                d["kb"].reshape(1, bb, tt_, KV_HEADS_B, HD_B), d["vb"].reshape(1, bb, tt_, KV_HEADS_B, HD_B),
                d["ki"].reshape(1, bb, tt_, IDX_DIM))

    return (y_p, y_s) + rows(pp_, b, t) + rows(sp, db, 1)
```

```python
import functools
import math

import numpy as np
import jax
import jax.numpy as jnp
from jax import lax
from jax.experimental import pallas as pl
from jax.experimental.pallas import tpu as pltpu

F32 = jnp.float32
BF16 = jnp.bfloat16
I32 = jnp.int32

D_MODEL = 2048
PAGE_SIZE = 128
H_A = 8
Q_LORA = 512
KV_LORA = 512
NOPE_DIM = 128
ROPE_DIM = 64
V_DIM = 128
MLA_SCALE = (NOPE_DIM + ROPE_DIM) ** -0.5
H_B = 8
HD_B = 128
KV_HEADS_B = 2
ROT_B = HD_B // 4
DSA_SCALE = HD_B ** -0.5
H_IDX = 16
IDX_DIM = 64
ROT_IDX = IDX_DIM // 4
IDX_SCALE = IDX_DIM ** -0.5
W_IDX_SCALE = H_IDX ** -0.5
TOPK_MAX = 256
PEER_HEADS = 8
N_KEYS = 128
PEER_DQ = 256
PEER_TOPK = 16
ROPE_THETA = 500000.0
RMS_EPS = 1e-6
LN_EPS = 1e-5
DEPTH = 1
ALPHA = (2 * DEPTH) ** 0.25

IN_SIZES = (Q_LORA, KV_LORA + ROPE_DIM, H_B * HD_B, KV_HEADS_B * HD_B, KV_HEADS_B * HD_B,
            H_IDX * IDX_DIM, IDX_DIM, H_IDX, 2 * D_MODEL)
IN_OFFSETS = tuple(int(o) for o in np.cumsum(IN_SIZES)[:-1])

LANE = 128
INT_MIN = -2 ** 31
VMEM_LIMIT = 56 * 1024 * 1024

O_Q = 0
O_C = O_Q + Q_LORA
O_KR = O_C + KV_LORA
O_KB = O_KR + LANE
O_VB = O_KB + KV_HEADS_B * HD_B
O_KI = O_VB + KV_HEADS_B * HD_B
O_WI = O_KI + LANE
P1_COLS = O_WI + LANE

NT_DIMS = (((1,), (1,)), ((), ()))


def _cparams(sem):
    return pltpu.CompilerParams(dimension_semantics=sem, vmem_limit_bytes=VMEM_LIMIT)


def _const_spec(shape):
    n = len(shape)
    return pl.BlockSpec(shape, lambda *_: (0,) * n)


def _rms(x, g):
    return x * lax.rsqrt(jnp.mean(x * x, axis=-1, keepdims=True) + RMS_EPS) * g


def _layer_norm(x, g, b):
    mu = jnp.mean(x, axis=-1, keepdims=True)
    d = x - mu
    var = jnp.mean(d * d, axis=-1, keepdims=True)
    return d * lax.rsqrt(var + LN_EPS) * g + b


def _rope(x, tab_ref, half):
    w = x.shape[-1]
    return (x * tab_ref[0] + pltpu.roll(x, w - half, 1) * tab_ref[1] + pltpu.roll(x, half, 1) * tab_ref[2])


def _rope_table(pos, width, head_dim, rot):
    half = rot // 2
    inv = jnp.power(ROPE_THETA, -jnp.arange(half, dtype=F32) / half)
    ang = pos.astype(F32)[:, None] * inv[None, :]
    cos, sin = jnp.cos(ang), jnp.sin(ang)
    t = pos.shape[0]
    rest = jnp.zeros((t, head_dim - rot), F32)
    zh = jnp.zeros((t, half), F32)
    c = jnp.concatenate([cos, cos, rest + 1.0], axis=1)
    s1 = jnp.concatenate([-sin, zh, rest], axis=1)
    s2 = jnp.concatenate([zh, sin, rest], axis=1)
    return jnp.stack([jnp.tile(a, (1, width // head_dim)) for a in (c, s1, s2)])


def _sort_key(s):
    bits = pltpu.bitcast(s + 0.0, I32)
    return jnp.where(bits < 0, bits ^ 0x7FFFFFFF, bits)


def _kth_largest_key(key, k):
    kf = float(k)

    def count_ge(c):
        return jnp.sum((key >= c).astype(F32), axis=1, keepdims=True)

    ans = jnp.where(count_ge(jnp.zeros_like(key[:, :1])) >= kf, 0, INT_MIN).astype(I32)

    def body(i, ans):
        cand = ans | jnp.left_shift(jnp.int32(1), 30 - i)
        return jnp.where(count_ge(cand) >= kf, cand, ans)

    return lax.fori_loop(0, 31, body, ans)


def _top_rows(vals, k):
    r = vals.shape[0]
    rid = lax.broadcasted_iota(I32, vals.shape, 0).astype(F32)
    tops, ids = [], []
    for _ in range(k):
        m = jnp.max(vals, axis=0, keepdims=True)
        idx = jnp.min(jnp.where(vals == m, rid, float(r)), axis=0, keepdims=True)
        vals = jnp.where(rid == idx, -jnp.inf, vals)
        tops.append(m)
        ids.append(idx)
    return jnp.concatenate(tops, axis=0), jnp.concatenate(ids, axis=0)


def _proj1_kernel(x_ref, w_ref, qg_ref, kvg_ref, ig_ref, ib_ref, tkr_ref, tkb_ref, tki_ref,
                  qn_ref, ckv_ref, kr_ref, kb_ref, vb_ref, ki_ref, wi_ref,
                  ckvb_ref, krb_ref, kbb_ref, vbb_ref, kib_ref):
    z = jnp.dot(x_ref[...].astype(BF16), w_ref[...], preferred_element_type=F32)
    qn_ref[...] = _rms(z[:, O_Q:O_Q + Q_LORA], qg_ref[...]).astype(BF16)
    ckv = _rms(z[:, O_C:O_C + KV_LORA], kvg_ref[...])
    ckv_ref[...] = ckv
    ckvb_ref[...] = ckv.astype(BF16)
    kr = _rope(z[:, O_KR:O_KR + LANE], tkr_ref, ROPE_DIM // 2)[:, :ROPE_DIM]
    kr_ref[...] = kr
    krb_ref[...] = kr.astype(BF16)
    kb = _rope(z[:, O_KB:O_KB + KV_HEADS_B * HD_B], tkb_ref, ROT_B // 2)
    kb_ref[...] = kb
    kbb_ref[...] = kb.astype(BF16)
    vb = z[:, O_VB:O_VB + KV_HEADS_B * HD_B]
    vb_ref[...] = vb
    vbb_ref[...] = vb.astype(BF16)
    zi = z[:, O_KI:O_KI + LANE]
    real = lax.broadcasted_iota(I32, zi.shape, 1) < IDX_DIM
    mu = jnp.sum(zi, axis=-1, keepdims=True) * (1.0 / IDX_DIM)
    d = jnp.where(real, zi - mu, 0.0)
    var = jnp.sum(d * d, axis=-1, keepdims=True) * (1.0 / IDX_DIM)
    ki = _rope(d * lax.rsqrt(var + LN_EPS) * ig_ref[...] + ib_ref[...], tki_ref, ROT_IDX // 2)[:, :IDX_DIM]
    ki_ref[...] = ki
    kib_ref[...] = ki.astype(BF16)
    wi_ref[...] = z[:, O_WI:O_WI + H_IDX] * W_IDX_SCALE


def _proj2_kernel(x_ref, w_ref, tqb_ref, tqi_ref, qb_ref, qi_ref):
    z = jnp.dot(x_ref[...].astype(BF16), w_ref[...], preferred_element_type=F32)
    nb = H_B * HD_B
    qb_ref[...] = _rope(z[:, :nb], tqb_ref, ROT_B // 2).astype(BF16)
    qi_ref[...] = (_rope(z[:, nb:], tqi_ref, ROT_IDX // 2) * IDX_SCALE).astype(BF16)


def _gate_kernel(x_ref, w_ref, g_ref):
    z = jnp.dot(x_ref[...].astype(BF16), w_ref[...], preferred_element_type=F32)
    g_ref[...] = jax.nn.sigmoid(z)


def _qpath_kernel(qn_ref, wuq_ref, wukt_ref, tqr_ref, ql_ref, qr_ref):
    qa = jnp.dot(qn_ref[...], wuq_ref[...], preferred_element_type=F32)
    n_nope = H_A * NOPE_DIM
    qr = _rope(qa[:, n_nope:], tqr_ref, ROPE_DIM // 2).astype(BF16)
    for h in range(H_A):
        qr_ref[0, h] = qr[:, h * ROPE_DIM:(h + 1) * ROPE_DIM]
        nope = qa[:, h * NOPE_DIM:(h + 1) * NOPE_DIM].astype(BF16)
        ql_ref[0, h] = jnp.dot(nope, wukt_ref[h], preferred_element_type=F32).astype(BF16)


def _tab_spec(tab, tm, tiles_per_batch):
    w = tab.shape[-1]
    if tab.shape[1] == 1:
        return pl.BlockSpec((3, 1, w), lambda i: (0, 0, 0))
    return pl.BlockSpec((3, tm, w), lambda i: (0, i % tiles_per_batch, 0))


def _project(x2, pos, n_batch, weights):
    n = x2.shape[0]
    t = pos.shape[0]
    tm = min(256, n if t == 1 else t)
    tpb = max(1, t // tm)
    w1, w2, wg, qg, kvg, ig, ib, wuq, wukt = weights
    tkr = _rope_table(pos, LANE, LANE, ROPE_DIM)
    tkb = _rope_table(pos, KV_HEADS_B * HD_B, HD_B, ROT_B)
    tki = _rope_table(pos, LANE, LANE, ROT_IDX)
    tqb = _rope_table(pos, H_B * HD_B, HD_B, ROT_B)
    tqi = _rope_table(pos, H_IDX * IDX_DIM, IDX_DIM, ROT_IDX)
    tqr = _rope_table(pos, H_A * ROPE_DIM, ROPE_DIM, ROPE_DIM)
    row = lambda w: pl.BlockSpec((tm, w), lambda i: (i, 0))
    xspec = row(D_MODEL)
    kvw = KV_HEADS_B * HD_B
    outs1 = pl.pallas_call(
        _proj1_kernel,
        grid=(n // tm,),
        in_specs=[xspec, _const_spec((D_MODEL, P1_COLS)), _const_spec((1, Q_LORA)), _const_spec((1, KV_LORA)),
                  _const_spec((1, LANE)), _const_spec((1, LANE)),
                  _tab_spec(tkr, tm, tpb), _tab_spec(tkb, tm, tpb), _tab_spec(tki, tm, tpb)],
        out_specs=[row(Q_LORA), row(KV_LORA), row(ROPE_DIM), row(kvw), row(kvw), row(IDX_DIM), row(H_IDX),
                   row(KV_LORA), row(ROPE_DIM), row(kvw), row(kvw), row(IDX_DIM)],
        out_shape=[jax.ShapeDtypeStruct((n, Q_LORA), BF16), jax.ShapeDtypeStruct((n, KV_LORA), F32),
                   jax.ShapeDtypeStruct((n, ROPE_DIM), F32), jax.ShapeDtypeStruct((n, kvw), F32),
                   jax.ShapeDtypeStruct((n, kvw), F32), jax.ShapeDtypeStruct((n, IDX_DIM), F32),
                   jax.ShapeDtypeStruct((n, H_IDX), F32),
                   jax.ShapeDtypeStruct((n, KV_LORA), BF16), jax.ShapeDtypeStruct((n, ROPE_DIM), BF16),
                   jax.ShapeDtypeStruct((n, kvw), BF16), jax.ShapeDtypeStruct((n, kvw), BF16),
                   jax.ShapeDtypeStruct((n, IDX_DIM), BF16)],
        compiler_params=_cparams(("parallel",)),
        name="proj1",
    )(x2, w1, qg, kvg, ig, ib, tkr, tkb, tki)
    qn, ckv, kr, kb, vb, ki, wi, ckv_b, kr_b, kb_b, vb_b, ki_b = outs1

    qb, qi = pl.pallas_call(
        _proj2_kernel,
        grid=(n // tm,),
        in_specs=[xspec, _const_spec(w2.shape), _tab_spec(tqb, tm, tpb), _tab_spec(tqi, tm, tpb)],
        out_specs=[row(H_B * HD_B), row(H_IDX * IDX_DIM)],
        out_shape=[jax.ShapeDtypeStruct((n, H_B * HD_B), BF16), jax.ShapeDtypeStruct((n, H_IDX * IDX_DIM), BF16)],
        compiler_params=_cparams(("parallel",)),
        name="proj2",
    )(x2, w2, tqb, tqi)

    tn = D_MODEL
    tg = min(512, n)
    gates = pl.pallas_call(
        _gate_kernel,
        grid=(2 * D_MODEL // tn, n // tg),
        in_specs=[pl.BlockSpec((tg, D_MODEL), lambda j, i: (i, 0)), pl.BlockSpec((D_MODEL, tn), lambda j, i: (0, j))],
        out_specs=pl.BlockSpec((tg, tn), lambda j, i: (i, j)),
        out_shape=jax.ShapeDtypeStruct((n, 2 * D_MODEL), F32),
        compiler_params=_cparams(("parallel", "parallel")),
        name="gates",
    )(x2, wg)

    t_eff = n // n_batch
    ql, qr = pl.pallas_call(
        _qpath_kernel,
        grid=(n // tm,),
        in_specs=[row(Q_LORA), _const_spec(wuq.shape), _const_spec(wukt.shape), _tab_spec(tqr, tm, tpb)],
        out_specs=[pl.BlockSpec((1, H_A, tm, KV_LORA), lambda i: (i // (t_eff // tm), 0, i % (t_eff // tm), 0)),
                   pl.BlockSpec((1, H_A, tm, ROPE_DIM), lambda i: (i // (t_eff // tm), 0, i % (t_eff // tm), 0))],
        out_shape=[jax.ShapeDtypeStruct((n_batch, H_A, t_eff, KV_LORA), BF16),
                   jax.ShapeDtypeStruct((n_batch, H_A, t_eff, ROPE_DIM), BF16)],
        compiler_params=_cparams(("parallel",)),
        name="qpath",
    )(qn, wuq, wukt, tqr)
    return dict(ckv=ckv, kr=kr, kb=kb, vb=vb, ki=ki, wi=wi, ckv_b=ckv_b, kr_b=kr_b, kb_b=kb_b, vb_b=vb_b,
                ki_b=ki_b, qb=qb, qi=qi, gates=gates, ql=ql, qr=qr)


def _mla_prompt_kernel(ql_ref, qr_ref, c_ref, kr_ref, o_ref, m_sc, l_sc, acc_sc, *, tq, tk):
    i, j = pl.program_id(1), pl.program_id(2)
    rows = H_A * tq

    @pl.when(j == 0)
    def _():
        m_sc[...] = jnp.full_like(m_sc, -jnp.inf)
        l_sc[...] = jnp.zeros_like(l_sc)
        acc_sc[...] = jnp.zeros_like(acc_sc)

    @pl.when(j * tk < (i + 1) * tq)
    def _():
        ql = ql_ref[0].reshape(rows, KV_LORA)
        qr = qr_ref[0].reshape(rows, ROPE_DIM)
        c = c_ref[0]
        s = (lax.dot_general(ql, c, NT_DIMS, preferred_element_type=F32)
             + lax.dot_general(qr, kr_ref[0], NT_DIMS, preferred_element_type=F32)) * MLA_SCALE
        q_pos = i * tq + lax.broadcasted_iota(I32, (H_A, tq, tk), 1).reshape(rows, tk)
        k_pos = j * tk + lax.broadcasted_iota(I32, (rows, tk), 1)
        s = jnp.where(k_pos <= q_pos, s, -jnp.inf)
        m_new = jnp.maximum(m_sc[...], jnp.max(s, axis=1, keepdims=True))
        a = jnp.exp(m_sc[...] - m_new)
        p = jnp.exp(s - m_new)
        l_sc[...] = a * l_sc[...] + jnp.sum(p, axis=1, keepdims=True)
        acc_sc[...] = a * acc_sc[...] + jnp.dot(p.astype(BF16), c, preferred_element_type=F32)
        m_sc[...] = m_new

    @pl.when(j == pl.num_programs(2) - 1)
    def _():
        o_ref[0] = (acc_sc[...] / l_sc[...]).astype(BF16).reshape(H_A, tq, KV_LORA)


def _mla_prompt(ql, qr, ckv_b, kr_b):
    b, _, t, _ = ql.shape
    tq = min(256, t)
    tk = min(512, t)
    last = lambda i, j: jnp.minimum(j, ((i + 1) * tq - 1) // tk)
    return pl.pallas_call(
        functools.partial(_mla_prompt_kernel, tq=tq, tk=tk),
        grid=(b, t // tq, t // tk),
        in_specs=[pl.BlockSpec((1, H_A, tq, KV_LORA), lambda bb, i, j: (bb, 0, i, 0)),
                  pl.BlockSpec((1, H_A, tq, ROPE_DIM), lambda bb, i, j: (bb, 0, i, 0)),
                  pl.BlockSpec((1, tk, KV_LORA), lambda bb, i, j: (bb, last(i, j), 0)),
                  pl.BlockSpec((1, tk, ROPE_DIM), lambda bb, i, j: (bb, last(i, j), 0))],
        out_specs=pl.BlockSpec((1, H_A, tq, KV_LORA), lambda bb, i, j: (bb, 0, i, 0)),
        out_shape=jax.ShapeDtypeStruct((b, H_A, t, KV_LORA), BF16),
        scratch_shapes=[pltpu.VMEM((H_A * tq, 1), F32), pltpu.VMEM((H_A * tq, 1), F32),
                        pltpu.VMEM((H_A * tq, KV_LORA), F32)],
        compiler_params=_cparams(("parallel", "parallel", "arbitrary")),
        name="mla_prompt",
    )(ql, qr, ckv_b, kr_b)


DSA_KEY_SPAN = 512


def _dsa_prompt_kernel(qi_ref, wi_ref, qb_ref, ki_ref, kb_ref, vb_ref, o_ref, *, tq, k_sel):
    i = pl.program_id(1)
    t_full = ki_ref.shape[1]
    span = min(DSA_KEY_SPAN, t_full)
    tiles_per_span = span // tq
    for v in range(t_full // span):
        pl.when(i // tiles_per_span == v)(
            functools.partial(_dsa_prompt_body, qi_ref, wi_ref, qb_ref, ki_ref, kb_ref, vb_ref, o_ref,
                              tq=tq, k_sel=k_sel, t_all=(v + 1) * span))


def _dsa_prompt_body(qi_ref, wi_ref, qb_ref, ki_ref, kb_ref, vb_ref, o_ref, *, tq, k_sel, t_all):
    i = pl.program_id(1)
    ki = ki_ref[0, :t_all, :]
    wi = wi_ref[...]
    score = jnp.zeros((tq, t_all), F32)
    for h in range(H_IDX):
        d = lax.dot_general(qi_ref[:, h * IDX_DIM:(h + 1) * IDX_DIM], ki, NT_DIMS, preferred_element_type=F32)
        score = score + wi[:, h:h + 1] * jnp.maximum(d, 0.0)
    q_pos = i * tq + lax.broadcasted_iota(I32, (tq, t_all), 0)
    k_pos = lax.broadcasted_iota(I32, (tq, t_all), 1)
    causal = k_pos <= q_pos
    key = jnp.where(causal, _sort_key(score), INT_MIN)
    thr = _kth_largest_key(key, k_sel)
    keep = (causal & (key >= thr))[None]
    hpg = H_B // KV_HEADS_B
    for g in range(KV_HEADS_B):
        qg = jnp.concatenate([qb_ref[:, (g * hpg + jj) * HD_B:(g * hpg + jj + 1) * HD_B] for jj in range(hpg)], axis=0)
        kg = kb_ref[0, :t_all, g * HD_B:(g + 1) * HD_B]
        vg = vb_ref[0, :t_all, g * HD_B:(g + 1) * HD_B]
        s = lax.dot_general(qg, kg, NT_DIMS, preferred_element_type=F32) * DSA_SCALE
        s = jnp.where(keep, s.reshape(hpg, tq, t_all), -jnp.inf).reshape(hpg * tq, t_all)
        p = jnp.exp(s - jnp.max(s, axis=1, keepdims=True))
        l = jnp.sum(p, axis=1, keepdims=True)
        o = (jnp.dot(p.astype(BF16), vg, preferred_element_type=F32) / l).astype(BF16)
        for jj in range(hpg):
            o_ref[:, (g * hpg + jj) * HD_B:(g * hpg + jj + 1) * HD_B] = o[jj * tq:(jj + 1) * tq]


def _dsa_prompt(qi, wi, qb, ki_b, kb_b, vb_b, b, t):
    tq = min(128, t)
    k_sel = min(TOPK_MAX, t // 4)
    tpb = t // tq
    row = lambda w: pl.BlockSpec((tq, w), lambda bb, i: (bb * tpb + i, 0))
    per_b = lambda w: pl.BlockSpec((1, t, w), lambda bb, i: (bb, 0, 0))
    kvw = KV_HEADS_B * HD_B
    return pl.pallas_call(
        functools.partial(_dsa_prompt_kernel, tq=tq, k_sel=k_sel),
        grid=(b, tpb),
        in_specs=[row(H_IDX * IDX_DIM), row(H_IDX), row(H_B * HD_B), per_b(IDX_DIM), per_b(kvw), per_b(kvw)],
        out_specs=row(H_B * HD_B),
        out_shape=jax.ShapeDtypeStruct((b * t, H_B * HD_B), BF16),
        compiler_params=_cparams(("parallel", "arbitrary")),
        name="dsa_prompt",
    )(qi, wi, qb, ki_b.reshape(b, t, IDX_DIM), kb_b.reshape(b, t, kvw), vb_b.reshape(b, t, kvw))


def _merge_kernel(ol_ref, ob_ref, g_ref, x_ref, wuv_ref, wao_ref, wbo_ref, wout_ref, lg_ref, lb_ref, h_ref):
    ta = jnp.concatenate([jnp.dot(ol_ref[0, h], wuv_ref[h], preferred_element_type=F32) for h in range(H_A)],
                         axis=1).astype(BF16)
    ya = jnp.dot(ta, wao_ref[...], preferred_element_type=F32)
    yb = jnp.dot(ob_ref[...], wbo_ref[...], preferred_element_type=F32)
    m = (g_ref[:, :D_MODEL] * ya + g_ref[:, D_MODEL:] * yb).astype(BF16)
    mix = jnp.dot(m, wout_ref[...], preferred_element_type=F32)
    h_ref[...] = _layer_norm(ALPHA * x_ref[...] + mix, lg_ref[...], lb_ref[...])


def _merge(o_lat, o_b, gates, x2, weights):
    b, _, t, _ = o_lat.shape
    tm = min(256, t)
    tpb = t // tm
    wuv, wao, wbo, wout, lg, lb = weights
    row = lambda w: pl.BlockSpec((tm, w), lambda i: (i, 0))
    once = lambda a: pl.BlockSpec(a.shape, lambda i: (0,) * a.ndim, pipeline_mode=pl.Buffered(1))
    return pl.pallas_call(
        _merge_kernel,
        grid=(b * tpb,),
        in_specs=[pl.BlockSpec((1, H_A, tm, KV_LORA), lambda i: (i // tpb, 0, i % tpb, 0)),
                  row(H_B * HD_B), row(2 * D_MODEL), row(D_MODEL),
                  once(wuv), once(wao), once(wbo), once(wout), once(lg), once(lb)],
        out_specs=row(D_MODEL),
        out_shape=jax.ShapeDtypeStruct((b * t, D_MODEL), F32),
        compiler_params=_cparams(("parallel",)),
        name="merge",
    )(o_lat, o_b, gates, x2, wuv, wao, wbo, wout, lg, lb)


def _peer_route_kernel(h_ref, wq_ref, sk_ref, e_ref, g_ref):
    q = jnp.dot(h_ref[...].astype(BF16), wq_ref[...], preferred_element_type=F32).astype(BF16)
    half = PEER_DQ // 2
    gates, experts = [], []
    for h in range(PEER_HEADS):
        sv, si = [], []
        for c in range(2):
            qhc = q[:, (2 * h + c) * half:(2 * h + c + 1) * half]
            s_t = lax.dot_general(sk_ref[h, c], qhc, NT_DIMS, preferred_element_type=F32)
            v, ix = _top_rows(s_t, PEER_TOPK)
            sv.append(v)
            si.append(ix)
        kk = PEER_TOPK
        tm = sv[0].shape[1]
        a8 = lax.broadcasted_iota(I32, (8, tm), 0)
        a16 = lax.broadcasted_iota(I32, (kk, tm), 0)
        cand = [sv[0] + sv[1][0:1]]
        cidx = [si[0] * float(N_KEYS) + si[1][0:1]]
        flat = [a16 * kk]
        for b in range(1, 8):
            ok = (a8 + 1) * (b + 1) <= kk
            cand.append(jnp.where(ok, sv[0][0:8] + sv[1][b:b + 1], -jnp.inf))
            cidx.append(si[0][0:8] * float(N_KEYS) + si[1][b:b + 1])
            flat.append(a8 * kk + b)
        cand.append(sv[0][0:1] + sv[1][8:kk])
        cidx.append(si[0][0:1] * float(N_KEYS) + si[1][8:kk])
        flat.append(a8 + 8)
        cand = jnp.concatenate(cand, axis=0)
        cidx = jnp.concatenate(cidx, axis=0)
        rid = jnp.concatenate(flat, axis=0).astype(F32)
        fv, fe = [], []
        for _ in range(PEER_TOPK):
            m = jnp.max(cand, axis=0, keepdims=True)
            pos = jnp.min(jnp.where(cand == m, rid, float(kk * kk)), axis=0, keepdims=True)
            hit = rid == pos
            fe.append(jnp.sum(jnp.where(hit, cidx, 0.0), axis=0, keepdims=True))
            cand = jnp.where(hit, -jnp.inf, cand)
            fv.append(m)
        fv = jnp.concatenate(fv, axis=0)
        p = jnp.exp(fv - fv[0:1])
        gates.append(p / jnp.sum(p, axis=0, keepdims=True))
        experts.append(jnp.concatenate(fe, axis=0))
    g_ref[...] = jnp.concatenate(gates, axis=0).T
    e_ref[...] = jnp.concatenate(experts, axis=0).T.astype(I32)


def _peer_route(h2, wq, sk):
    n = h2.shape[0]
    tm = min(128, n)
    once = lambda a: pl.BlockSpec(a.shape, lambda i: (0,) * a.ndim)
    return pl.pallas_call(
        _peer_route_kernel,
        grid=(n // tm,),
        in_specs=[pl.BlockSpec((tm, D_MODEL), lambda i: (i, 0)), once(wq), once(sk)],
        out_specs=[pl.BlockSpec((tm, N_SEL), lambda i: (i, 0)), pl.BlockSpec((tm, N_SEL), lambda i: (i, 0))],
        out_shape=[jax.ShapeDtypeStruct((n, N_SEL), I32), jax.ShapeDtypeStruct((n, N_SEL), F32)],
        compiler_params=_cparams(("parallel",)),
        name="peer_route",
    )(h2, wq, sk)


N_SEL = PEER_HEADS * PEER_TOPK
PEER_TT = 32
PEER_SLOTS = 4
PEER_AHEAD = PEER_SLOTS - 1
IDX_RING = 3
PEER_CHUNK = 16
HALF_D = D_MODEL // 2


def _pack_bf16_pairs(a):
    h = a.shape[1] // 2
    bits = lambda x: lax.bitcast_convert_type(x.astype(BF16), jnp.uint16).astype(jnp.uint32)
    return lax.bitcast_convert_type(bits(a[:, :h]) | (bits(a[:, h:]) << 16), I32)


def _low_bf16(w):
    return pltpu.bitcast(w << 16, F32)


def _high_bf16(w):
    return pltpu.bitcast(w & -65536, F32)


def _peer_expert_kernel(e_hbm, h_ref, g_ref, tab_hbm, lg_ref, lb_ref, y_ref,
                        idx_sm, *rest, n_total, n_steps):
    buf, (po_sc, sem_i, sem_g) = rest[:PEER_SLOTS], rest[PEER_SLOTS:]
    i = pl.program_id(0)
    tt = PEER_TT

    def idx_copy(step):
        ring = step % IDX_RING
        return pltpu.make_async_copy(e_hbm.at[step], idx_sm.at[pl.ds(ring * tt, tt)], sem_i.at[ring])

    def row_copy(e, slot, k):
        return pltpu.make_async_copy(tab_hbm.at[e], buf[slot].at[pl.ds(k, 1)], sem_g.at[slot])

    def idx_row(n):
        n = jnp.minimum(n, n_total - 1)
        return ((n // tt) % IDX_RING) * tt + n % tt

    def issue(row, slot, k0, k1):
        for k in range(k0, k1):
            row_copy(idx_sm[row, k], slot, k).start(priority=k % 2)

    def wait(slot):
        other = buf[(slot + 1) % PEER_SLOTS]
        pltpu.make_async_copy(other, buf[slot], sem_g.at[slot]).wait()

    @pl.when(i == 0)
    def _():
        idx_copy(0).start()
        if n_steps > 1:
            idx_copy(1).start()
        idx_copy(0).wait()
        for t in range(PEER_AHEAD):
            issue(idx_row(t), t, 0, N_SEL)

    @pl.when(i + 1 < n_steps)
    def _():
        idx_copy(i + 1).wait()

    @pl.when(i + 2 < n_steps)
    def _():
        idx_copy(i + 2).start()

    n_chunks = N_SEL // PEER_CHUNK
    per_chunk = N_SEL // (2 * n_chunks)

    def token(t, slot):
        wait(slot)
        row = idx_row(i * tt + t + PEER_AHEAD)
        ahead_slot = (slot + PEER_AHEAD) % PEER_SLOTS
        x = h_ref[pl.ds(t, 1), :]
        x_lo, x_hi = x[:, :HALF_D], x[:, HALF_D:]
        words = buf[slot]
        hk = []
        for c in range(n_chunks):
            wu = words[c * PEER_CHUNK:(c + 1) * PEER_CHUNK, :HALF_D]
            hk.append(jnp.sum(_low_bf16(wu) * x_lo + _high_bf16(wu) * x_hi, axis=1, keepdims=True))
            issue(row, ahead_slot, c * per_chunk, (c + 1) * per_chunk)
        hk = jnp.concatenate(hk, axis=0)
        gcol = jnp.transpose(jnp.broadcast_to(g_ref[pl.ds(t, 1), :], (N_SEL, N_SEL)))[:, 0:1]
        a = gcol * (0.5 * hk * (1.0 + lax.erf(hk * math.sqrt(0.5))))
        acc_lo = jnp.zeros((8, HALF_D), F32)
        acc_hi = jnp.zeros((8, HALF_D), F32)
        for c in range(n_chunks):
            wv = words[c * PEER_CHUNK:(c + 1) * PEER_CHUNK, HALF_D:]
            ac = a[c * PEER_CHUNK:(c + 1) * PEER_CHUNK]
            for r in range(0, PEER_CHUNK, 8):
                acc_lo = acc_lo + _low_bf16(wv[r:r + 8]) * ac[r:r + 8]
                acc_hi = acc_hi + _high_bf16(wv[r:r + 8]) * ac[r:r + 8]
            issue(row, ahead_slot, (n_chunks + c) * per_chunk, (n_chunks + c + 1) * per_chunk)
        po_sc[pl.ds(t, 1), :] = jnp.concatenate(
            [jnp.sum(acc_lo, axis=0, keepdims=True), jnp.sum(acc_hi, axis=0, keepdims=True)], axis=1)

    def group(g, carry):
        for j in range(PEER_SLOTS):
            token(g * PEER_SLOTS + j, j)
        return carry

    lax.fori_loop(0, tt // PEER_SLOTS, group, 0)

    @pl.when(i == n_steps - 1)
    def _():
        for j in range(PEER_AHEAD):
            wait(j)

    y_ref[...] = _layer_norm(ALPHA * h_ref[...] + po_sc[...], lg_ref[...], lb_ref[...])


def _peer_experts(eidx, gate, h2, table, lg, lb):
    n = h2.shape[0]
    tt = PEER_TT
    assert n % tt == 0 and tt % PEER_SLOTS == 0
    steps = n // tt
    row = lambda w: pl.BlockSpec((tt, w), lambda i: (i, 0))
    anyspec = pl.BlockSpec(memory_space=pl.ANY)
    return pl.pallas_call(
        functools.partial(_peer_expert_kernel, n_total=n, n_steps=steps),
        grid=(steps,),
        in_specs=[anyspec, row(D_MODEL), row(N_SEL), anyspec,
                  _const_spec((1, D_MODEL)), _const_spec((1, D_MODEL))],
        out_specs=row(D_MODEL),
        out_shape=jax.ShapeDtypeStruct((n, D_MODEL), F32),
        scratch_shapes=[pltpu.SMEM((IDX_RING * tt, N_SEL), I32),
                        *[pltpu.VMEM((N_SEL, D_MODEL), I32) for _ in range(PEER_SLOTS)],
                        pltpu.VMEM((tt, D_MODEL), F32),
                        pltpu.SemaphoreType.DMA((IDX_RING,)),
                        pltpu.SemaphoreType.DMA((PEER_SLOTS,))],
        compiler_params=_cparams(("arbitrary",)),
        name="peer_experts",
    )(eidx.reshape(steps, tt, N_SEL), h2, gate, table, lg, lb)


def _peer_layer(h2, peer_w):
    wq, sk, table, lg, lb = peer_w
    eidx, gate = _peer_route(h2, wq, sk)
    return _peer_experts(eidx, gate, h2, table, lg, lb)


PAGES_PER_STEP = 32


def _stream_pages(pt_ref, pools, bufs, sem, page_window):
    pp = PAGES_PER_STEP
    nps = pl.num_programs(1)
    step = pl.program_id(0) * nps + pl.program_id(1)
    total = pl.num_programs(0) * nps
    slot = step % 2

    def start(s, into):
        seq, group = s // nps, s % nps
        for j in range(pp):
            page = pt_ref[seq, group * pp + j]
            for a, pool in enumerate(pools):
                pltpu.make_async_copy(pool.at[page], page_window(a, bufs[a].at[into], j),
                                      sem.at[a, into]).start(priority=j % 2)

    @pl.when(step == 0)
    def _():
        start(step, slot)

    @pl.when(step + 1 < total)
    def _():
        start(step + 1, 1 - slot)

    for a in range(len(pools)):
        pltpu.make_async_copy(bufs[a].at[1 - slot], bufs[a].at[slot], sem.at[a, slot]).wait()
    return slot


def _rows_window(rows_per_page):
    return lambda buf, j: buf.at[pl.ds(j * rows_per_page, rows_per_page)]


def _lanes_window(buf, j):
    return buf.at[:, pl.ds(j * PAGE_SIZE, PAGE_SIZE)]


def _sample_pass1_kernel(pt_ref, ql_ref, qr_ref, qi_ref, wi_ref, cn_ref, krn_ref, c_hbm, kr_hbm, ki_hbm,
                         o_ref, sc_ref, m_sc, l_sc, acc_sc, cbuf, krbuf, kibuf, sem):
    p_id = pl.program_id(1)
    windows = (_rows_window(PAGE_SIZE), _lanes_window, _lanes_window)
    slot = _stream_pages(pt_ref, (c_hbm, kr_hbm, ki_hbm), (cbuf, krbuf, kibuf), sem,
                         lambda a, buf, j: windows[a](buf, j))

    @pl.when(p_id == 0)
    def _():
        m_sc[...] = jnp.full_like(m_sc, -jnp.inf)
        l_sc[...] = jnp.zeros_like(l_sc)
        acc_sc[...] = jnp.zeros_like(acc_sc)

    ql, qr = ql_ref[0], qr_ref[0]

    def absorb(s, weighted_values):
        s = s * MLA_SCALE
        m_new = jnp.maximum(m_sc[...], jnp.max(s, axis=1, keepdims=True))
        a = jnp.exp(m_sc[...] - m_new)
        p = jnp.exp(s - m_new)
        l_sc[...] = a * l_sc[...] + jnp.sum(p, axis=1, keepdims=True)
        acc_sc[...] = a * acc_sc[...] + weighted_values(p.astype(BF16))
        m_sc[...] = m_new

    c = cbuf[slot].astype(BF16)
    kr_t = krbuf[slot].astype(BF16)
    absorb(lax.dot_general(ql, c, NT_DIMS, preferred_element_type=F32)
           + jnp.dot(qr, kr_t, preferred_element_type=F32),
           lambda p: jnp.dot(p, c, preferred_element_type=F32))
    ki_t = kibuf[slot].astype(BF16)
    d = jnp.dot(qi_ref[0], ki_t, preferred_element_type=F32)
    sc_ref[0] = jnp.sum(wi_ref[0] * jnp.maximum(d, 0.0), axis=0, keepdims=True)

    @pl.when(p_id == pl.num_programs(1) - 1)
    def _():
        cn = cn_ref[0].astype(F32)
        s_new = (jnp.sum(ql.astype(F32) * cn, axis=1, keepdims=True)
                 + jnp.sum(qr.astype(F32) * krn_ref[0].astype(F32), axis=1, keepdims=True))
        absorb(s_new, lambda p: p.astype(F32) * cn)
        o_ref[0] = (acc_sc[...] / l_sc[...]).astype(BF16)


def _sample_pass1(page_table, ql, qr, qi3, wi3, ckv_new, kr_new, c_pool, kr_pool, ki_pool):
    db, n_pages = page_table.shape
    pp = PAGES_PER_STEP
    per_b = lambda a: pl.BlockSpec((1,) + a.shape[1:], lambda b, p, pt: (b,) + (0,) * (a.ndim - 1))
    anyspec = pl.BlockSpec(memory_space=pl.ANY)
    span = pp * PAGE_SIZE
    grid_spec = pltpu.PrefetchScalarGridSpec(
        num_scalar_prefetch=1,
        grid=(db, n_pages // pp),
        in_specs=[per_b(ql), per_b(qr), per_b(qi3), per_b(wi3), per_b(ckv_new), per_b(kr_new),
                  anyspec, anyspec, anyspec],
        out_specs=[pl.BlockSpec((1, H_A, KV_LORA), lambda b, p, pt: (b, 0, 0)),
                   pl.BlockSpec((1, 1, span), lambda b, p, pt: (b, 0, p))],
        scratch_shapes=[pltpu.VMEM((H_A, 1), F32), pltpu.VMEM((H_A, 1), F32), pltpu.VMEM((H_A, KV_LORA), F32),
                        pltpu.VMEM((2, span, KV_LORA), F32), pltpu.VMEM((2, ROPE_DIM, span), F32),
                        pltpu.VMEM((2, IDX_DIM, span), F32), pltpu.SemaphoreType.DMA((3, 2))],
    )
    return pl.pallas_call(
        _sample_pass1_kernel,
        grid_spec=grid_spec,
        out_shape=[jax.ShapeDtypeStruct((db, H_A, KV_LORA), BF16),
                   jax.ShapeDtypeStruct((db, 1, n_pages * PAGE_SIZE), F32)],
        compiler_params=_cparams(("arbitrary", "arbitrary")),
        name="sample_pass1",
    )(page_table, ql, qr, qi3, wi3, ckv_new, kr_new, c_pool, kr_pool, ki_pool)


def _sample_select_kernel(sc_ref, qi_ref, wi_ref, kin_ref, thr_ref, new_ref, *, k_sel):
    past = sc_ref[...]
    qi = qi_ref[...].astype(F32)
    kn = kin_ref[...].astype(F32)
    d = jnp.sum(qi * kn, axis=2)
    s_new = jnp.sum(wi_ref[...] * jnp.maximum(d, 0.0), axis=1, keepdims=True)
    key_new = _sort_key(s_new)
    lane0 = lax.broadcasted_iota(I32, (past.shape[0], LANE), 1) == 0
    key = jnp.concatenate([_sort_key(past), jnp.where(lane0, key_new, INT_MIN)], axis=1)
    thr = _kth_largest_key(key, k_sel)
    thr_ref[...] = thr
    new_ref[...] = (key_new >= thr).astype(I32)


def _sample_select(scores, qi3, wi2, ki_new3, k_sel):
    db, past = scores.shape
    rows = min(8, db)
    return pl.pallas_call(
        functools.partial(_sample_select_kernel, k_sel=k_sel),
        grid=(db // rows,),
        in_specs=[pl.BlockSpec((rows, past), lambda i: (i, 0)),
                  pl.BlockSpec((rows, H_IDX, IDX_DIM), lambda i: (i, 0, 0)),
                  pl.BlockSpec((rows, H_IDX), lambda i: (i, 0)),
                  pl.BlockSpec((rows, 1, IDX_DIM), lambda i: (i, 0, 0))],
        out_specs=[pl.BlockSpec((rows, 1), lambda i: (i, 0)), pl.BlockSpec((rows, 1), lambda i: (i, 0))],
        out_shape=[jax.ShapeDtypeStruct((db, 1), I32), jax.ShapeDtypeStruct((db, 1), I32)],
        compiler_params=_cparams(("parallel",)),
        name="sample_select",
    )(scores, qi3, wi2, ki_new3)


def _sample_pass2_kernel(pt_ref, thr_ref, new_ref, qb_ref, sc_ref, kn_ref, vn_ref, k_hbm, v_hbm,
                         o_ref, m_sc, l_sc, acc_sc, kbuf, vbuf, sem):
    pp = PAGES_PER_STEP
    b, p_id = pl.program_id(0), pl.program_id(1)
    hpg = H_B // KV_HEADS_B
    window = _rows_window(PAGE_SIZE * KV_HEADS_B)
    slot = _stream_pages(pt_ref, (k_hbm, v_hbm), (kbuf, vbuf), sem, lambda a, buf, j: window(buf, j))

    @pl.when(p_id == 0)
    def _():
        m_sc[...] = jnp.full_like(m_sc, -jnp.inf)
        l_sc[...] = jnp.zeros_like(l_sc)
        acc_sc[...] = jnp.zeros_like(acc_sc)

    def absorb(g, s, keep, weighted_values):
        rows = slice(g * hpg, (g + 1) * hpg)
        s = jnp.where(keep, s * DSA_SCALE, -jnp.inf)
        m_old = m_sc[rows]
        m_new = jnp.maximum(m_old, jnp.max(s, axis=1, keepdims=True))
        m_safe = jnp.where(m_new == -jnp.inf, 0.0, m_new)
        a = jnp.exp(m_old - m_safe)
        p = jnp.exp(s - m_safe)
        l_sc[rows] = a * l_sc[rows] + jnp.sum(p, axis=1, keepdims=True)
        acc_sc[rows] = a * acc_sc[rows] + weighted_values(p.astype(BF16))
        m_sc[rows] = m_new

    keep = _sort_key(sc_ref[0]) >= thr_ref[b]
    for g in range(KV_HEADS_B):
        own = pl.ds(g, pp * PAGE_SIZE, stride=KV_HEADS_B)
        k = kbuf[slot, own, :].astype(BF16)
        v = vbuf[slot, own, :].astype(BF16)
        absorb(g, lax.dot_general(qb_ref[0, g * hpg:(g + 1) * hpg], k, NT_DIMS, preferred_element_type=F32),
               keep, lambda p, v=v: jnp.dot(p, v, preferred_element_type=F32))

    @pl.when(p_id == pl.num_programs(1) - 1)
    def _():
        for g in range(KV_HEADS_B):
            cols = slice(g * HD_B, (g + 1) * HD_B)
            kn = kn_ref[0, :, cols].astype(F32)
            vn = vn_ref[0, :, cols].astype(F32)
            s_new = jnp.sum(qb_ref[0, g * hpg:(g + 1) * hpg].astype(F32) * kn, axis=1, keepdims=True)
            absorb(g, s_new, new_ref[b] > 0, lambda p, vn=vn: p.astype(F32) * vn)
        o_ref[0] = (acc_sc[...] / l_sc[...]).astype(BF16)


def _sample_pass2(page_table, thr, sel_new, qb3, scores3, kb_new, vb_new, k_pool, v_pool):
    db, n_pages = page_table.shape
    pp = PAGES_PER_STEP
    kvw = KV_HEADS_B * HD_B
    per_b = lambda a: pl.BlockSpec((1,) + a.shape[1:], lambda b, p, *_: (b,) + (0,) * (a.ndim - 1))
    anyspec = pl.BlockSpec(memory_space=pl.ANY)
    group_rows = pp * PAGE_SIZE * KV_HEADS_B
    grid_spec = pltpu.PrefetchScalarGridSpec(
        num_scalar_prefetch=3,
        grid=(db, n_pages // pp),
        in_specs=[per_b(qb3), pl.BlockSpec((1, 1, pp * PAGE_SIZE), lambda b, p, *_: (b, 0, p)),
                  per_b(kb_new), per_b(vb_new), anyspec, anyspec],
        out_specs=pl.BlockSpec((1, H_B, HD_B), lambda b, p, *_: (b, 0, 0)),
        scratch_shapes=[pltpu.VMEM((H_B, 1), F32), pltpu.VMEM((H_B, 1), F32), pltpu.VMEM((H_B, HD_B), F32),
                        pltpu.VMEM((2, group_rows, HD_B), F32), pltpu.VMEM((2, group_rows, HD_B), F32),
                        pltpu.SemaphoreType.DMA((2, 2))],
    )
    return pl.pallas_call(
        _sample_pass2_kernel,
        grid_spec=grid_spec,
        out_shape=jax.ShapeDtypeStruct((db, H_B, HD_B), BF16),
        compiler_params=_cparams(("arbitrary", "arbitrary")),
        name="sample_pass2",
    )(page_table, thr, sel_new, qb3, scores3, kb_new, vb_new, k_pool, v_pool)


def kernel(x_prompt, x_sample, cache_kv_latent, cache_k_rope, cache_k, cache_v, cache_idx_k, page_table, w_in, mla_q_norm, mla_w_uq, mla_kv_norm, mla_w_uk, mla_w_uv, mla_w_o, idx_k_norm_g, idx_k_norm_b, dsa_w_o, w_out, ln1_g, ln1_b, peer_w_q, peer_sub_keys, peer_u, peer_v, ln2_g, ln2_b):
    b, t, _ = x_prompt.shape
    db, ts, _ = x_sample.shape
    assert ts == 1, "the sample group decodes one token per sequence"
    n_pages = page_table.shape[1]
    past_len = n_pages * PAGE_SIZE
    l = 0

    wq_, wkv_, wbq_, wbk_, wbv_, wiq_, wik_, wiw_, wg_ = jnp.split(w_in[l], IN_OFFSETS, axis=1)
    padc = lambda a: jnp.pad(a, ((0, 0), (0, LANE - a.shape[1])))
    w1 = jnp.concatenate([wq_, wkv_[:, :KV_LORA], padc(wkv_[:, KV_LORA:]), wbk_, wbv_, padc(wik_), padc(wiw_)],
                         axis=1).astype(BF16)
    w2 = jnp.concatenate([wbq_, wiq_], axis=1).astype(BF16)
    wg = wg_.astype(BF16)
    wuq = jnp.concatenate([mla_w_uq[l][:, :, :NOPE_DIM].reshape(Q_LORA, H_A * NOPE_DIM),
                           mla_w_uq[l][:, :, NOPE_DIM:].reshape(Q_LORA, H_A * ROPE_DIM)], axis=1).astype(BF16)
    wukt = jnp.transpose(mla_w_uk[l], (1, 2, 0)).astype(BF16)
    row1 = lambda a: a.reshape(1, -1).astype(F32)
    pad1 = lambda a: jnp.pad(a.astype(F32), (0, LANE - a.shape[0])).reshape(1, LANE)
    proj_w = (w1, w2, wg, row1(mla_q_norm[l]), row1(mla_kv_norm[l]), pad1(idx_k_norm_g[l]), pad1(idx_k_norm_b[l]),
              wuq, wukt)
    merge_w = (jnp.transpose(mla_w_uv[l], (1, 0, 2)).astype(BF16), mla_w_o[l].astype(BF16),
               dsa_w_o[l].astype(BF16), w_out[l].astype(BF16), row1(ln1_g[l]), row1(ln1_b[l]))
    peer_table = jnp.concatenate([_pack_bf16_pairs(peer_u[l]), _pack_bf16_pairs(peer_v[l])], axis=1)
    peer_table = peer_table.reshape(peer_table.shape[0], 1, D_MODEL)
    peer_w = (peer_w_q[l].astype(BF16), peer_sub_keys[l].astype(BF16), peer_table, row1(ln2_g[l]), row1(ln2_b[l]))
    kvw = KV_HEADS_B * HD_B

    xs2 = x_sample.reshape(db, D_MODEL)
    sp = _project(xs2, jnp.full((1,), past_len, I32), 1, proj_w)
    ql_s = jnp.transpose(sp["ql"][0], (1, 0, 2))
    qr_s = jnp.transpose(sp["qr"][0], (1, 0, 2))
    qi3 = sp["qi"].reshape(db, H_IDX, IDX_DIM)
    o_lat_s, scores = _sample_pass1(
        page_table, ql_s, qr_s, qi3, sp["wi"].reshape(db, H_IDX, 1),
        sp["ckv_b"].reshape(db, 1, KV_LORA), sp["kr_b"].reshape(db, 1, ROPE_DIM),
        cache_kv_latent[l], jnp.swapaxes(cache_k_rope[l], 1, 2), jnp.swapaxes(cache_idx_k[l], 1, 2))
    k_sel_s = min(TOPK_MAX, (past_len + 1) // 4)
    thr, sel_new = _sample_select(scores.reshape(db, past_len), qi3, sp["wi"], sp["ki_b"].reshape(db, 1, IDX_DIM),
                                  k_sel_s)
    o_b_s = _sample_pass2(page_table, thr.reshape(db), sel_new.reshape(db), sp["qb"].reshape(db, H_B, HD_B), scores,
                          sp["kb_b"].reshape(db, 1, kvw), sp["vb_b"].reshape(db, 1, kvw),
                          cache_k[l].reshape(-1, PAGE_SIZE * KV_HEADS_B, HD_B),
                          cache_v[l].reshape(-1, PAGE_SIZE * KV_HEADS_B, HD_B))
    o_lat_s4 = jnp.transpose(o_lat_s, (1, 0, 2)).reshape(1, H_A, db, KV_LORA)
    h_s = _merge(o_lat_s4, o_b_s.reshape(db, H_B * HD_B), sp["gates"], xs2, merge_w)
    y_s = _peer_layer(h_s, peer_w).reshape(db, 1, D_MODEL)

    xp2 = x_prompt.reshape(b * t, D_MODEL)
    pp_ = _project(xp2, jnp.arange(t, dtype=I32), b, proj_w)
    o_lat_p = _mla_prompt(pp_["ql"], pp_["qr"], pp_["ckv_b"].reshape(b, t, KV_LORA), pp_["kr_b"].reshape(b, t, ROPE_DIM))
    o_b_p = _dsa_prompt(pp_["qi"], pp_["wi"], pp_["qb"], pp_["ki_b"], pp_["kb_b"], pp_["vb_b"], b, t)
    h_p = _merge(o_lat_p, o_b_p, pp_["gates"], xp2, merge_w)
    y_p = _peer_layer(h_p, peer_w).reshape(b, t, D_MODEL)

    def rows(d, bb, tt_):
        return (d["ckv"].reshape(1, bb, tt_, KV_LORA), d["kr"].reshape(1, bb, tt_, ROPE_DIM),
                d["kb"].reshape(1, bb, tt_, KV_HEADS_B, HD_B), d["vb"].reshape(1, bb, tt_, KV_HEADS_B, HD_B),
                d["ki"].reshape(1, bb, tt_, IDX_DIM))

    return (y_p, y_s) + rows(pp_, b, t) + rows(sp, db, 1)
```
